```python
import jax
import jax.numpy as jnp
from jax import lax
import numpy as np

D_MODEL = 2048
BATCH = 4
SEQ = 4096
DEPTH = 2

GRID_W = 64
CTX_LEN = 256
D_MIX = D_MODEL
D_SSM = D_MIX // 2
D_ATTN = D_MIX - D_SSM
SSM_HEAD_DIM = 64
SSM_HEADS = D_SSM // SSM_HEAD_DIM
SSM_STATE = 128
SSM_GROUPS = 2
SSM_BC = SSM_GROUPS * SSM_STATE
D_XBC = D_SSM + 2 * SSM_BC
SSM_CONV = 5
SSM_CHUNK = 128
ATTN_HEAD_DIM = 128
ATTN_HEADS = D_ATTN // ATTN_HEAD_DIM
ATTN_KV_HEADS = 2
KV_DIM = ATTN_KV_HEADS * ATTN_HEAD_DIM
WINDOW = 128
ATTN_BLOCK = 128
ROPE_BASE = 10000.0
D_FF = 4 * D_MODEL
N_MOD = 6
D_IN_PROJ = D_XBC + D_SSM + 2 * SSM_HEADS + D_ATTN + 2 * KV_DIM
EPS = 1e-6
DT_MIN = 0.001
DT_MAX = 0.1

kernel_name = 'hybrid_ssd_swa_dit_block'


def rms_norm(x, g):
    xf = x.astype(jnp.float32)
    y = xf * lax.rsqrt(jnp.mean(xf * xf, axis=-1, keepdims=True) + EPS)
    return (y * g.astype(jnp.float32)).astype(x.dtype)


def modulation(cond, w_mod, b_mod):
    m = jax.nn.silu(cond) @ w_mod + b_mod
    return jnp.split(m, N_MOD, axis=-1)


def split_projection(p):
    c1 = D_XBC
    c2 = c1 + D_SSM
    c3 = c2 + 2 * SSM_HEADS
    c4 = c3 + D_ATTN
    c5 = c4 + KV_DIM
    return jnp.split(p, [c1, c2, c3, c4, c5], axis=-1)


def centred_depthwise_conv(u, w, bias):
    y = lax.conv_general_dilated(
        u, w[:, None, :].astype(u.dtype), window_strides=(1,),
        padding=[(SSM_CONV // 2, SSM_CONV // 2)],
        dimension_numbers=('NWC', 'WIO', 'NWC'),
        feature_group_count=u.shape[-1])
    return y + bias


def ssd_chunked(xh, dt, a, bm, cm, h0):
    dtype = xh.dtype
    f32 = jnp.float32
    b, L, H, P = xh.shape
    G, N, Q = SSM_GROUPS, SSM_STATE, SSM_CHUNK
    R = H // G
    nc = L // Q
    log_dec = (dt.astype(f32) * a.astype(f32)).reshape(b, nc, Q, G, R)
    a_cs = jnp.cumsum(log_dec, axis=2)
    xd = (xh.astype(f32) * dt.astype(f32)[..., None]).reshape(b, nc, Q, G, R, P)
    bc = bm.astype(f32).reshape(b, nc, Q, G, N)
    cc = cm.astype(f32).reshape(b, nc, Q, G, N)
    seg = a_cs[:, :, :, None] - a_cs[:, :, None, :]
    lower = jnp.tril(jnp.ones((Q, Q), dtype=bool))[None, None, :, :, None, None]
    decay = jnp.exp(jnp.where(lower, seg, -jnp.inf))
    cb = jnp.einsum('bclgn,bcsgn->bclsg', cc, bc)
    y_diag = jnp.einsum('bclsgr,bcsgrp->bclgrp', cb[..., None] * decay, xd)
    decay_to_end = jnp.exp(a_cs[:, :, -1:] - a_cs)
    states = jnp.einsum('bcsgn,bcsgrp->bcgrpn', bc, xd * decay_to_end[..., None])
    chunk_decay = jnp.exp(a_cs[:, :, -1])

    def step(h, inp):
        st, dec = inp
        return h * dec[..., None, None] + st, h

    h_final, h_in = lax.scan(
        step, h0.astype(f32).reshape(b, G, R, P, N),
        (jnp.moveaxis(states, 1, 0), jnp.moveaxis(chunk_decay, 1, 0)))
    h_in = jnp.moveaxis(h_in, 0, 1)
    y_off = jnp.einsum('bclgn,bcgrpn->bclgrp', cc, h_in) * jnp.exp(a_cs)[..., None]
    y = (y_diag + y_off).reshape(b, L, H, P)
    return y.astype(dtype), h_final.reshape(b, H, P, N)


def bidirectional_ssd(xbc, dt_raw, z, a_log, dt_bias, d_skip, norm_w, h0_fwd, h0_bwd):
    b, L, _ = xbc.shape
    xs, bm, cm = jnp.split(xbc, [D_SSM, D_SSM + SSM_BC], axis=-1)
    xh = xs.reshape(b, L, SSM_HEADS, SSM_HEAD_DIM)
    bm = bm.reshape(b, L, SSM_GROUPS, SSM_STATE)
    cm = cm.reshape(b, L, SSM_GROUPS, SSM_STATE)
    a = -jnp.exp(a_log.astype(jnp.float32))
    dt = jax.nn.softplus(dt_raw.astype(jnp.float32).reshape(b, L, 2, SSM_HEADS)
                         + dt_bias.astype(jnp.float32))
    flip = lambda t: jnp.flip(t, axis=1)
    y_f, h_f = ssd_chunked(xh, dt[:, :, 0], a[0], bm, cm, h0_fwd)
    y_b, h_b = ssd_chunked(flip(xh), flip(dt[:, :, 1]), a[1], flip(bm), flip(cm), h0_bwd)
    y = y_f + flip(y_b) + xh * d_skip[:, None].astype(xh.dtype)
    y = y.reshape(b, L, D_SSM) * jax.nn.silu(z)
    return rms_norm(y, norm_w), h_f, h_b


def axial_rope(x, pos_row, pos_col):
    n_freq = ATTN_HEAD_DIM // 4
    inv_freq = ROPE_BASE ** (-jnp.arange(n_freq, dtype=jnp.float32) / n_freq)

    def rot(u, pos):
        ang = pos[:, None] * inv_freq[None, :]
        cos = jnp.cos(ang)[None, :, None, :]
        sin = jnp.sin(ang)[None, :, None, :]
        u1, u2 = jnp.split(u, 2, axis=-1)
        return jnp.concatenate([u1 * cos - u2 * sin, u2 * cos + u1 * sin], axis=-1)

    xr, xcol = jnp.split(x.astype(jnp.float32), 2, axis=-1)
    return jnp.concatenate([rot(xr, pos_row), rot(xcol, pos_col)], axis=-1).astype(x.dtype)


def windowed_attention(q, k, v, kc, vc, sink):
    b, L = q.shape[:2]
    T = ATTN_BLOCK
    nb = L // T
    R = ATTN_HEADS // ATTN_KV_HEADS
    scale = ATTN_HEAD_DIM ** -0.5
    qb = q.reshape(b, nb, T, ATTN_KV_HEADS, R, ATTN_HEAD_DIM)
    pad = ((0, 0), (T, T), (0, 0), (0, 0))
    kp = jnp.pad(k, pad).reshape(b, nb + 2, T, ATTN_KV_HEADS, ATTN_HEAD_DIM)
    vp = jnp.pad(v, pad).reshape(b, nb + 2, T, ATTN_KV_HEADS, ATTN_HEAD_DIM)
    kband = jnp.concatenate([kp[:, :-2], kp[:, 1:-1], kp[:, 2:]], axis=2)
    vband = jnp.concatenate([vp[:, :-2], vp[:, 1:-1], vp[:, 2:]], axis=2)
    s_loc = jnp.einsum('bnqhrd,bnkhd->bnhrqk', qb, kband).astype(jnp.float32) * scale
    blk = jnp.arange(nb)[:, None, None]
    qpos = blk * T + jnp.arange(T)[None, :, None]
    kpos = (blk - 1) * T + jnp.arange(3 * T)[None, None, :]
    valid = (jnp.abs(qpos - kpos) <= WINDOW) & (kpos >= 0) & (kpos < L)
    s_loc = jnp.where(valid[None, :, None, None], s_loc, -jnp.inf)
    s_ctx = jnp.einsum('bnqhrd,bkhd->bnhrqk', qb, kc).astype(jnp.float32) * scale
    s_sink = jnp.broadcast_to(
        sink.astype(jnp.float32).reshape(ATTN_KV_HEADS, R)[None, None, :, :, None, None],
        s_loc.shape[:-1] + (1,))
    p = jax.nn.softmax(jnp.concatenate([s_loc, s_ctx, s_sink], axis=-1), axis=-1)
    p = p.astype(v.dtype)
    n_loc = 3 * T
    n_ctx = kc.shape[1]
    out = (jnp.einsum('bnhrqk,bnkhd->bnqhrd', p[..., :n_loc], vband)
           + jnp.einsum('bnhrqk,bkhd->bnqhrd', p[..., n_loc:n_loc + n_ctx], vc))
    return out.reshape(b, L, ATTN_HEADS * ATTN_HEAD_DIM)


def context_attention(qc, kc, vc, sink):
    b, Lc = qc.shape[:2]
    R = ATTN_HEADS // ATTN_KV_HEADS
    qg = qc.reshape(b, Lc, ATTN_KV_HEADS, R, ATTN_HEAD_DIM)
    s = jnp.einsum('bqhrd,bkhd->bhrqk', qg, kc).astype(jnp.float32) * ATTN_HEAD_DIM ** -0.5
    s_sink = jnp.broadcast_to(
        sink.astype(jnp.float32).reshape(ATTN_KV_HEADS, R)[None, :, :, None, None],
        s.shape[:-1] + (1,))
    p = jax.nn.softmax(jnp.concatenate([s, s_sink], axis=-1), axis=-1).astype(vc.dtype)
    out = jnp.einsum('bhrqk,bkhd->bqhrd', p[..., :Lc], vc)
    return out.reshape(b, Lc, ATTN_HEADS * ATTN_HEAD_DIM)


def hybrid_mixer(h, hc, w_in, conv_w, conv_b, a_log, dt_bias, d_skip, ssm_norm,
                 attn_sink, w_out, ctx_out):
    b, L, _ = h.shape
    Lc = hc.shape[1]
    xbc, z, dt_raw, q, k, v = split_projection(h @ w_in)
    xbc_c, z_c, dt_c, q_c, k_c, v_c = split_projection(hc @ w_in)
    xbc = jax.nn.silu(centred_depthwise_conv(xbc, conv_w, conv_b))
    xbc_c = jax.nn.silu(centred_depthwise_conv(xbc_c, conv_w, conv_b))
    h0 = jnp.zeros((b, SSM_HEADS, SSM_HEAD_DIM, SSM_STATE), jnp.float32)
    y_ssm_c, h_fwd, h_bwd = bidirectional_ssd(xbc_c, dt_c, z_c, a_log, dt_bias, d_skip,
                                              ssm_norm, h0, h0)
    y_ssm, _, _ = bidirectional_ssd(xbc, dt_raw, z, a_log, dt_bias, d_skip, ssm_norm,
                                    h_fwd, h_bwd)
    rows = L // GRID_W
    pos_row = jnp.broadcast_to(jnp.arange(rows, dtype=jnp.float32)[:, None],
                               (rows, GRID_W)).reshape(-1)
    pos_col = jnp.broadcast_to(jnp.arange(GRID_W, dtype=jnp.float32)[None, :],
                               (rows, GRID_W)).reshape(-1)
    q = axial_rope(q.reshape(b, L, ATTN_HEADS, ATTN_HEAD_DIM), pos_row, pos_col)
    k = axial_rope(k.reshape(b, L, ATTN_KV_HEADS, ATTN_HEAD_DIM), pos_row, pos_col)
    v = v.reshape(b, L, ATTN_KV_HEADS, ATTN_HEAD_DIM)
    kc = k_c.reshape(b, Lc, ATTN_KV_HEADS, ATTN_HEAD_DIM)
    vc = v_c.reshape(b, Lc, ATTN_KV_HEADS, ATTN_HEAD_DIM)
    y_attn = windowed_attention(q, k, v, kc, vc, attn_sink)
    out = jnp.concatenate([y_ssm, y_attn], axis=-1) @ w_out
    if ctx_out:
        y_attn_c = context_attention(q_c.reshape(b, Lc, ATTN_HEADS, ATTN_HEAD_DIM), kc, vc,
                                     attn_sink)
        out_c = jnp.concatenate([y_ssm_c, y_attn_c], axis=-1) @ w_out
    else:
        out_c = None
    return out, out_c


def squared_relu_mlp(h, w1, w2):
    return jnp.square(jax.nn.relu(h @ w1)) @ w2


def setup_inputs(seed: int = 0) -> dict:
    key = jax.random.key(seed)
    ks = jax.random.split(key, 24)
    f32 = jnp.float32
    nrm = lambda k, shape, s: jax.random.normal(k, shape, f32) * s
    dt0 = jnp.exp(jax.random.uniform(ks[13], (DEPTH, 2, SSM_HEADS), f32,
                                     np.log(DT_MIN), np.log(DT_MAX)))
    return {
        'x': nrm(ks[0], (BATCH, SEQ, D_MODEL), 1.0),
        'c': nrm(ks[1], (BATCH, D_MODEL), 1.0),
        'ctx': nrm(ks[2], (BATCH, CTX_LEN, D_MODEL), 1.0),
        'c_ctx': nrm(ks[3], (D_MODEL,), 1.0),
        'w_mod': nrm(ks[4], (DEPTH, D_MODEL, N_MOD * D_MODEL), 0.5 * D_MODEL ** -0.5),
        'b_mod': nrm(ks[5], (DEPTH, N_MOD * D_MODEL), 0.02),
        'g_pre_mix': 1.0 + nrm(ks[6], (DEPTH, D_MODEL), 0.05),
        'g_post_mix': 1.0 + nrm(ks[7], (DEPTH, D_MODEL), 0.05),
        'g_pre_mlp': 1.0 + nrm(ks[8], (DEPTH, D_MODEL), 0.05),
        'g_post_mlp': 1.0 + nrm(ks[9], (DEPTH, D_MODEL), 0.05),
        'w_in': nrm(ks[10], (DEPTH, D_MODEL, D_IN_PROJ), D_MODEL ** -0.5),
        'conv_w': nrm(ks[11], (DEPTH, SSM_CONV, D_XBC), SSM_CONV ** -0.5),
        'conv_b': nrm(ks[12], (DEPTH, D_XBC), 0.02),
        'a_log': jnp.log(jax.random.uniform(ks[14], (DEPTH, 2, SSM_HEADS), f32, 1.0, 16.0)),
        'dt_bias': dt0 + jnp.log(-jnp.expm1(-dt0)),
        'd_skip': 1.0 + nrm(ks[15], (DEPTH, SSM_HEADS), 0.1),
        'ssm_norm': 1.0 + nrm(ks[16], (DEPTH, D_SSM), 0.05),
        'attn_sink': nrm(ks[17], (DEPTH, ATTN_HEADS), 0.5),
        'w_out': nrm(ks[18], (DEPTH, D_MIX, D_MODEL), D_MIX ** -0.5),
        'w_ff1': nrm(ks[19], (DEPTH, D_MODEL, D_FF), D_MODEL ** -0.5),
        'w_ff2': nrm(ks[20], (DEPTH, D_FF, D_MODEL), D_FF ** -0.5),
    }


def reference(x, c, ctx, c_ctx, w_mod, b_mod, g_pre_mix, g_post_mix, g_pre_mlp,
              g_post_mlp, w_in, conv_w, conv_b, a_log, dt_bias, d_skip, ssm_norm,
              attn_sink, w_out, w_ff1, w_ff2):
    xc = ctx
    for i in range(DEPTH):
        last = i == DEPTH - 1
        sh_m, sc_m, g_m, sh_f, sc_f, g_f = [m[:, None, :] for m in
                                            modulation(c, w_mod[i], b_mod[i])]
        csh_m, csc_m, cg_m, csh_f, csc_f, cg_f = modulation(c_ctx, w_mod[i], b_mod[i])
        h = rms_norm(x, g_pre_mix[i]) * (1.0 + sc_m) + sh_m
        hc = rms_norm(xc, g_pre_mix[i]) * (1.0 + csc_m) + csh_m
        mix, mix_c = hybrid_mixer(h, hc, w_in[i], conv_w[i], conv_b[i], a_log[i],
                                  dt_bias[i], d_skip[i], ssm_norm[i], attn_sink[i],
                                  w_out[i], not last)
        x = x + g_m * rms_norm(mix, g_post_mix[i])
        f = squared_relu_mlp(rms_norm(x, g_pre_mlp[i]) * (1.0 + sc_f) + sh_f,
                             w_ff1[i], w_ff2[i])
        x = x + g_f * rms_norm(f, g_post_mlp[i])
        if not last:
            xc = xc + cg_m * rms_norm(mix_c, g_post_mix[i])
            fc = squared_relu_mlp(rms_norm(xc, g_pre_mlp[i]) * (1.0 + csc_f) + csh_f,
                                  w_ff1[i], w_ff2[i])
            xc = xc + cg_f * rms_norm(fc, g_post_mlp[i])
    return x
```

```python
import functools

import jax
import jax.numpy as jnp
from jax import lax
from jax.experimental import pallas as pl
from jax.experimental.pallas import tpu as pltpu

F32 = jnp.float32
BF16 = jnp.bfloat16

N_MOD = 6
SSM_HEAD_DIM = 64
SSM_HEADS = 16
SSM_STATE = 128
SSM_GROUPS = 2
SSM_CONV = 5
SSM_CHUNK = 128
ATTN_HEAD_DIM = 128
ATTN_HEADS = 8
ATTN_KV_HEADS = 2
ATTN_BLOCK = 128
GRID_W = 64
ROPE_BASE = 10000.0
EPS = 1e-6

D_SSM = SSM_HEADS * SSM_HEAD_DIM
SSM_BC = SSM_GROUPS * SSM_STATE
D_XBC = D_SSM + 2 * SSM_BC
D_ATTN = ATTN_HEADS * ATTN_HEAD_DIM
KV_DIM = ATTN_KV_HEADS * ATTN_HEAD_DIM
D_MAIN = D_XBC + D_SSM + D_ATTN + 2 * KV_DIM
HEADS_PER_GROUP = SSM_HEADS // SSM_GROUPS
GROUP_COLS = HEADS_PER_GROUP * SSM_HEAD_DIM

LANES = 128
BF16_SUBLANES = 16
VMEM_LIMIT_BYTES = 56 * 1024 * 1024


def _cparams(sem):
    return pltpu.CompilerParams(dimension_semantics=sem, vmem_limit_bytes=VMEM_LIMIT_BYTES)


def _rms(xf, g):
    ms = jnp.mean(xf * xf, axis=-1, keepdims=True)
    return xf * lax.rsqrt(ms + EPS) * g


def _silu(v):
    return v * jax.nn.sigmoid(v)


def _resident(shape):
    nd = len(shape)
    return pl.BlockSpec(shape, lambda *_: (0,) * nd, pipeline_mode=pl.Buffered(1))


def _mod_kernel(c_ref, w_ref, b_ref, o_ref):
    s = _silu(c_ref[...]).astype(BF16)
    o_ref[0] = jnp.dot(s, w_ref[0].astype(BF16), preferred_element_type=F32) + b_ref[0]


def _modulation(cond, w_mod, b_mod):
    depth, d, n = w_mod.shape
    tn = 1024
    return pl.pallas_call(
        _mod_kernel,
        grid=(depth, n // tn),
        in_specs=[pl.BlockSpec((8, d), lambda l, j: (0, 0)),
                  pl.BlockSpec((1, d, tn), lambda l, j: (l, 0, j)),
                  pl.BlockSpec((1, 1, tn), lambda l, j: (l, 0, j))],
        out_specs=pl.BlockSpec((1, 8, tn), lambda l, j: (l, 0, j)),
        out_shape=jax.ShapeDtypeStruct((depth, 8, n), F32),
        compiler_params=_cparams(("arbitrary", "arbitrary")),
        name="modulation",
    )(cond, w_mod, b_mod.reshape(depth, 1, n))


def _rope(v, cos, sin):
    lane = lax.broadcasted_iota(jnp.int32, v.shape, 1)
    partner = jnp.where((lane % 64) < 32, pltpu.roll(v, 96, 1), pltpu.roll(v, 32, 1))
    return v * cos + partner * sin


def _inproj_kernel(x_ref, g_ref, sc_ref, sh_ref, w_ref, wdt_ref, cos_ref, sin_ref,
                   xbc_ref, z_ref, q_ref, k_ref, v_ref, dt_ref, h_scr, *, rope):
    h = _rms(x_ref[0], g_ref[...]) * (1.0 + sc_ref[0]) + sh_ref[0]
    h_scr[...] = h.astype(BF16)

    def proj(c0, n):
        return jnp.dot(h_scr[...], w_ref[:, c0:c0 + n], preferred_element_type=F32)

    seg = 512
    for s in range(D_XBC // seg):
        xbc_ref[0, :, s * seg:(s + 1) * seg] = proj(s * seg, seg).astype(BF16)
    for s in range(D_SSM // seg):
        z_ref[0, :, s * seg:(s + 1) * seg] = proj(D_XBC + s * seg, seg).astype(BF16)

    scale = ATTN_HEAD_DIM ** -0.5
    hd = ATTN_HEAD_DIM
    q0 = D_XBC + D_SSM
    for s in range(D_ATTN // seg):
        acc = proj(q0 + s * seg, seg)
        for hh in range(seg // hd):
            v = acc[:, hh * hd:(hh + 1) * hd]
            if rope:
                v = _rope(v, cos_ref[...], sin_ref[...])
            q_ref[0, :, s * seg + hh * hd:s * seg + (hh + 1) * hd] = (v * scale).astype(BF16)
    acc = proj(q0 + D_ATTN, KV_DIM)
    for hh in range(ATTN_KV_HEADS):
        v = acc[:, hh * hd:(hh + 1) * hd]
        if rope:
            v = _rope(v, cos_ref[...], sin_ref[...])
        k_ref[0, :, hh * hd:(hh + 1) * hd] = v.astype(BF16)
    v_ref[0] = proj(q0 + D_ATTN + KV_DIM, KV_DIM).astype(BF16)
    dt_ref[0] = jnp.dot(h_scr[...], wdt_ref[...], preferred_element_type=F32)


def _inproj(x, g, sc, sh, w_main, w_dt, cos, sin, *, rope, tm):
    b, l, d = x.shape
    row = lambda w: pl.BlockSpec((1, tm, w), lambda bi, i: (bi, i, 0))
    vec = pl.BlockSpec((1, 1, d), lambda bi, i: (bi, 0, 0))
    tab = pl.BlockSpec((tm, LANES), lambda bi, i: (i, 0))
    shapes = [(D_XBC, BF16), (D_SSM, BF16), (D_ATTN, BF16), (KV_DIM, BF16), (KV_DIM, BF16), (LANES, F32)]
    return pl.pallas_call(
        functools.partial(_inproj_kernel, rope=rope),
        grid=(b, l // tm),
        in_specs=[row(d), _resident((1, d)), vec, vec, _resident((d, D_MAIN)), _resident((d, LANES)), tab, tab],
        out_specs=[row(w) for w, _ in shapes],
        out_shape=[jax.ShapeDtypeStruct((b, l, w), dt) for w, dt in shapes],
        scratch_shapes=[pltpu.VMEM((tm, d), BF16)],
        compiler_params=_cparams(("arbitrary", "arbitrary")),
        name="inproj_rope" if rope else "inproj",
    )(x, g, sc, sh, w_main, w_dt, cos, sin)


def _conv_kernel(u_ref, prev_ref, next_ref, w_ref, b_ref, o_ref, ext_scr, *, t):
    i = pl.program_id(1)
    n = pl.num_programs(1)
    halo = SSM_CONV // 2
    pad = 8
    h16 = BF16_SUBLANES
    prev = prev_ref[0, h16 - pad:h16, :].astype(F32)
    nxt = next_ref[0, 0:pad, :].astype(F32)
    ext_scr[0:pad, :] = jnp.where(i > 0, prev, 0.0)
    ext_scr[pad:pad + t, :] = u_ref[0].astype(F32)
    ext_scr[pad + t:pad + t + pad, :] = jnp.where(i < n - 1, nxt, 0.0)
    acc = b_ref[...] + w_ref[0:1, :] * ext_scr[pad - halo:pad - halo + t, :]
    for k in range(1, SSM_CONV):
        acc = acc + w_ref[k:k + 1, :] * ext_scr[pad - halo + k:pad - halo + k + t, :]
    o_ref[0] = _silu(acc).astype(BF16)


def _conv_silu(u, w, bias, *, t):
    b, l, c = u.shape
    h16 = BF16_SUBLANES
    nh = l // h16
    per = t // h16
    return pl.pallas_call(
        functools.partial(_conv_kernel, t=t),
        grid=(b, l // t),
        in_specs=[pl.BlockSpec((1, t, c), lambda bi, i: (bi, i, 0)),
                  pl.BlockSpec((1, h16, c), lambda bi, i: (bi, jnp.maximum(i * per - 1, 0), 0)),
                  pl.BlockSpec((1, h16, c), lambda bi, i: (bi, jnp.minimum((i + 1) * per, nh - 1), 0)),
                  _resident((SSM_CONV, c)), _resident((1, c))],
        out_specs=pl.BlockSpec((1, t, c), lambda bi, i: (bi, i, 0)),
        out_shape=jax.ShapeDtypeStruct((b, l, c), BF16),
        scratch_shapes=[pltpu.VMEM((t + 16, c), F32)],
        compiler_params=_cparams(("arbitrary", "arbitrary")),
        name="conv_silu",
    )(u, u, u, w, bias)


def _split3(v):
    hi = v.astype(BF16)
    r1 = v - hi.astype(F32)
    mid = r1.astype(BF16)
    lo = (r1 - mid.astype(F32)).astype(BF16)
    return hi, mid, lo


def _ssd_kernel(xbc_ref, dt_ref, bias_ref, alog_ref, h0_ref, y_ref, hf_ref, e_scr, *, rev):
    c = pl.program_id(1)
    q = SSM_CHUNK
    off = SSM_HEADS if rev else 0

    @pl.when(c == 0)
    def _():
        hf_ref[0] = h0_ref[0]
        r = lax.broadcasted_iota(jnp.int32, e_scr.shape, 0)
        col = lax.broadcasted_iota(jnp.int32, e_scr.shape, 1)
        e_scr[...] = (r == col // SSM_HEAD_DIM + off).astype(BF16)

    x = dt_ref[0] + bias_ref[...]
    dt = jnp.maximum(x, 0.0) + jnp.log1p(jnp.exp(-jnp.abs(x)))
    a = -jnp.exp(alog_ref[...])
    dta = dt * a

    ri = lax.broadcasted_iota(jnp.int32, (q, q), 0)
    ci = lax.broadcasted_iota(jnp.int32, (q, q), 1)
    mask = (ci >= ri) if rev else (ci <= ri)
    tri = mask.astype(BF16)
    cs = sum(jnp.dot(tri, part, preferred_element_type=F32) for part in _split3(dta))
    tot = cs[0:1] if rev else cs[q - 1:q]
    cs_t = cs.T
    dt_t = dt.T

    ecs = jnp.exp(cs)
    ecs_hi = ecs.astype(BF16)
    ecs_lo = (ecs - ecs_hi.astype(F32)).astype(BF16)
    w_end = (dt * jnp.exp(tot - cs)).astype(BF16)
    exp_all = jnp.dot(jnp.concatenate([w_end, ecs_hi, ecs_lo], axis=0), e_scr[...],
                      preferred_element_type=F32)
    wx = exp_all[0:q]
    ecsx = exp_all[q:2 * q] + exp_all[2 * q:3 * q]
    cdec = ecsx[0:1] if rev else ecsx[q - 1:q]

    lane_lo = lax.broadcasted_iota(jnp.int32, (q, LANES), 1) < SSM_HEAD_DIM
    nt = (((1,), (1,)), ((), ()))
    for g in range(SSM_GROUPS):
        gc = slice(g * GROUP_COLS, (g + 1) * GROUP_COLS)
        bg = xbc_ref[0, :, D_SSM + g * SSM_STATE:D_SSM + (g + 1) * SSM_STATE]
        cg = xbc_ref[0, :, D_SSM + SSM_BC + g * SSM_STATE:D_SSM + SSM_BC + (g + 1) * SSM_STATE]
        cb = lax.dot_general(cg, bg, nt, preferred_element_type=F32)
        s_in = hf_ref[0, :, gc]
        y_off = jnp.dot(cg, s_in.astype(BF16), preferred_element_type=F32) * ecsx[:, gc]
        xg = xbc_ref[0, :, gc]
        xw = (xg.astype(F32) * wx[:, gc]).astype(BF16)
        bg_t = bg.astype(F32).T.astype(BF16)
        hf_ref[0, :, gc] = s_in * cdec[:, gc] + jnp.dot(bg_t, xw, preferred_element_type=F32)

        for jp in range(HEADS_PER_GROUP // 2):
            pair = g * (HEADS_PER_GROUP // 2) + jp

            def head_mat(h):
                seg = cs[:, off + h:off + h + 1] - cs_t[off + h:off + h + 1, :]
                dec = jnp.exp(jnp.where(mask, seg, -jnp.inf))
                return (cb * dec * dt_t[off + h:off + h + 1, :]).astype(BF16)

            lhs = jnp.concatenate([head_mat(2 * pair), head_mat(2 * pair + 1)], axis=1)
            xp = xbc_ref[0, :, pair * LANES:(pair + 1) * LANES]
            zero = jnp.zeros_like(xp)
            rhs = jnp.concatenate([jnp.where(lane_lo, xp, zero), jnp.where(lane_lo, zero, xp)], axis=0)
            y_d = jnp.dot(lhs, rhs, preferred_element_type=F32)
            y_ref[0, :, pair * LANES:(pair + 1) * LANES] = (
                y_d + y_off[:, jp * LANES:(jp + 1) * LANES]).astype(BF16)


def _ssd(xbc, dt_raw, dt_bias_row, a_log_row, h0, *, rev):
    b, l, _ = xbc.shape
    q = SSM_CHUNK
    nc = l // q
    cidx = (lambda c: nc - 1 - c) if rev else (lambda c: c)
    st = pl.BlockSpec((1, SSM_STATE, D_SSM), lambda bi, c: (bi, 0, 0))
    return pl.pallas_call(
        functools.partial(_ssd_kernel, rev=rev),
        grid=(b, nc),
        in_specs=[pl.BlockSpec((1, q, D_XBC), lambda bi, c: (bi, cidx(c), 0)),
                  pl.BlockSpec((1, q, LANES), lambda bi, c: (bi, cidx(c), 0)),
                  _resident((1, LANES)), _resident((1, LANES)), st],
        out_specs=[pl.BlockSpec((1, q, D_SSM), lambda bi, c: (bi, cidx(c), 0)), st],
        out_shape=[jax.ShapeDtypeStruct((b, l, D_SSM), BF16),
                   jax.ShapeDtypeStruct((b, SSM_STATE, D_SSM), F32)],
        scratch_shapes=[pltpu.VMEM((LANES, D_SSM), BF16)],
        compiler_params=_cparams(("arbitrary", "arbitrary")),
        name="ssd_bwd" if rev else "ssd_fwd",
    )(xbc, dt_raw, dt_bias_row, a_log_row, h0)


def _attn_kernel(*refs, band):
    if band:
        q_ref, kp_ref, kc_ref, kn_ref, vp_ref, vc_ref, vn_ref, kx_ref, vx_ref, sink_ref, o_ref = refs
    else:
        q_ref, kx_ref, vx_ref, sink_ref, o_ref = refs
    i = pl.program_id(1)
    nb = pl.num_programs(1)
    t = ATTN_BLOCK
    hd = ATTN_HEAD_DIM
    rep = ATTN_HEADS // ATTN_KV_HEADS
    lc = kx_ref.shape[1]
    nk = (3 * t if band else 0) + lc
    nt = (((1,), (1,)), ((), ()))

    if band:
        rpos = lax.broadcasted_iota(jnp.int32, (rep * t, nk), 0) % t
        col = lax.broadcasted_iota(jnp.int32, (rep * t, nk), 1)
        lo = rpos + jnp.where(i > 0, 0, 2 * t)
        hi = rpos + 2 * t - jnp.where(i < nb - 1, 0, 2 * t)
        valid = ~(((col < t) & (col < lo)) | ((col >= 2 * t) & (col < 3 * t) & (col > hi)))
    rblk = lax.broadcasted_iota(jnp.int32, (rep * t, 1), 0) // t

    for g in range(ATTN_KV_HEADS):
        ks = slice(g * hd, (g + 1) * hd)
        qg = jnp.concatenate([q_ref[0, :, (g * rep + r) * hd:(g * rep + r + 1) * hd] for r in range(rep)], axis=0)
        if band:
            k_all = jnp.concatenate([kp_ref[0, :, ks], kc_ref[0, :, ks], kn_ref[0, :, ks], kx_ref[0, :, ks]], axis=0)
            v_all = jnp.concatenate([vp_ref[0, :, ks], vc_ref[0, :, ks], vn_ref[0, :, ks], vx_ref[0, :, ks]], axis=0)
        else:
            k_all = kx_ref[0, :, ks]
            v_all = vx_ref[0, :, ks]
        s = lax.dot_general(qg, k_all, nt, preferred_element_type=F32)
        if band:
            s = jnp.where(valid, s, -jnp.inf)
        sink = jnp.zeros((rep * t, 1), F32)
        for r in range(rep):
            sink = jnp.where(rblk == r, sink_ref[:, g * rep + r:g * rep + r + 1], sink)
        m = jnp.maximum(jnp.max(s, axis=-1, keepdims=True), sink)
        p = jnp.exp(s - m)
        denom = jnp.sum(p, axis=-1, keepdims=True) + jnp.exp(sink - m)
        o = jnp.dot(p.astype(BF16), v_all, preferred_element_type=F32) / denom
        for r in range(rep):
            o_ref[0, :, (g * rep + r) * hd:(g * rep + r + 1) * hd] = o[r * t:(r + 1) * t].astype(BF16)


def _attention(q, k, v, kx, vx, sink_row, *, band):
    b, l, _ = q.shape
    t = ATTN_BLOCK
    nb = l // t
    lc = kx.shape[1]
    qs = pl.BlockSpec((1, t, D_ATTN), lambda bi, i: (bi, i, 0))
    ctx = pl.BlockSpec((1, lc, KV_DIM), lambda bi, i: (bi, 0, 0))
    sk = _resident((1, LANES))
    if band:
        blk = lambda f: pl.BlockSpec((1, t, KV_DIM), lambda bi, i: (bi, f(i), 0))
        prv = blk(lambda i: jnp.maximum(i - 1, 0))
        cur = blk(lambda i: i)
        nxt = blk(lambda i: jnp.minimum(i + 1, nb - 1))
        in_specs = [qs, prv, cur, nxt, prv, cur, nxt, ctx, ctx, sk]
        args = (q, k, k, k, v, v, v, kx, vx, sink_row)
    else:
        in_specs = [qs, ctx, ctx, sk]
        args = (q, kx, vx, sink_row)
    return pl.pallas_call(
        functools.partial(_attn_kernel, band=band),
        grid=(b, nb),
        in_specs=in_specs,
        out_specs=qs,
        out_shape=jax.ShapeDtypeStruct((b, l, D_ATTN), BF16),
        compiler_params=_cparams(("arbitrary", "arbitrary")),
        name="attn_band" if band else "attn_ctx",
    )(*args)


def _outproj_kernel(yf_ref, yb_ref, xs_ref, z_ref, ya_ref, x_ref, w_ref, dsk_ref, gn_ref, gpost_ref,
                    gate_ref, gpre_ref, scf_ref, shf_ref, xo_ref, hm_ref):
    y = yf_ref[0].astype(F32) + yb_ref[0].astype(F32) + xs_ref[0].astype(F32) * dsk_ref[...]
    y = y * _silu(z_ref[0].astype(F32))
    y = _rms(y, gn_ref[...]).astype(BF16)
    mix = (jnp.dot(y, w_ref[0:D_SSM, :], preferred_element_type=F32)
           + jnp.dot(ya_ref[0], w_ref[D_SSM:, :], preferred_element_type=F32))
    xn = x_ref[0] + gate_ref[0] * _rms(mix, gpost_ref[...])
    xo_ref[0] = xn
    hm_ref[0] = (_rms(xn, gpre_ref[...]) * (1.0 + scf_ref[0]) + shf_ref[0]).astype(BF16)


def _outproj(yf, yb, xbc, z, ya, x, w_out, dskip_row, ssm_norm, g_post, gate, g_pre, sc_f, sh_f, *, tm):
    b, l, d = x.shape
    row = lambda w: pl.BlockSpec((1, tm, w), lambda bi, i: (bi, i, 0))
    vec = pl.BlockSpec((1, 1, d), lambda bi, i: (bi, 0, 0))
    return pl.pallas_call(
        _outproj_kernel,
        grid=(b, l // tm),
        in_specs=[row(D_SSM), row(D_SSM), row(D_SSM), row(D_SSM), row(D_ATTN), row(d),
                  _resident(w_out.shape), _resident((1, D_SSM)), _resident((1, D_SSM)), _resident((1, d)),
                  vec, _resident((1, d)), vec, vec],
        out_specs=[row(d), row(d)],
        out_shape=[jax.ShapeDtypeStruct((b, l, d), F32), jax.ShapeDtypeStruct((b, l, d), BF16)],
        compiler_params=_cparams(("arbitrary", "arbitrary")),
        name="outproj",
    )(yf, yb, xbc, z, ya, x, w_out, dskip_row, ssm_norm, g_post, gate, g_pre, sc_f, sh_f)


def _mlp_kernel(h_ref, x_ref, w1_ref, w2_ref, g_ref, gate_ref, xo_ref, acc_ref):
    k = pl.program_id(2)
    a = jnp.dot(h_ref[0], w1_ref[...], preferred_element_type=F32)
    a = jnp.square(jnp.maximum(a, 0.0)).astype(BF16)
    part = jnp.dot(a, w2_ref[...], preferred_element_type=F32)

    @pl.when(k == 0)
    def _():
        acc_ref[...] = part

    @pl.when(k > 0)
    def _():
        acc_ref[...] += part

    @pl.when(k == pl.num_programs(2) - 1)
    def _():
        xo_ref[0] = x_ref[0] + gate_ref[0] * _rms(acc_ref[...], g_ref[...])


def _mlp(h, x, w1, w2, g_post, gate, *, tm, tf):
    b, l, d = x.shape
    dff = w1.shape[1]
    row = pl.BlockSpec((1, tm, d), lambda bi, i, k: (bi, i, 0))
    return pl.pallas_call(
        _mlp_kernel,
        grid=(b, l // tm, dff // tf),
        in_specs=[row, row,
                  pl.BlockSpec((d, tf), lambda bi, i, k: (0, k)),
                  pl.BlockSpec((tf, d), lambda bi, i, k: (k, 0)),
                  _resident((1, d)),
                  pl.BlockSpec((1, 1, d), lambda bi, i, k: (bi, 0, 0))],
        out_specs=row,
        out_shape=jax.ShapeDtypeStruct((b, l, d), F32),
        scratch_shapes=[pltpu.VMEM((tm, d), F32)],
        compiler_params=_cparams(("arbitrary", "arbitrary", "arbitrary")),
        name="mlp",
    )(h, x, w1, w2, g_post, gate)


def _rope_tables(l):
    n_freq = ATTN_HEAD_DIM // 4
    inv_freq = ROPE_BASE ** (-jnp.arange(n_freq, dtype=F32) / n_freq)
    t = jnp.arange(l, dtype=jnp.int32)
    ang_r = (t // GRID_W).astype(F32)[:, None] * inv_freq[None, :]
    ang_c = (t % GRID_W).astype(F32)[:, None] * inv_freq[None, :]
    cos = jnp.concatenate([jnp.cos(ang_r)] * 2 + [jnp.cos(ang_c)] * 2, axis=-1)
    sin = jnp.concatenate([-jnp.sin(ang_r), jnp.sin(ang_r), -jnp.sin(ang_c), jnp.sin(ang_c)], axis=-1)
    return cos, sin


def _lane_row(v):
    return jnp.pad(v.reshape(1, -1).astype(F32), ((0, 0), (0, LANES - v.size)))


def kernel(x, c, ctx, c_ctx, w_mod, b_mod, g_pre_mix, g_post_mix, g_pre_mlp, g_post_mlp, w_in, conv_w,
           conv_b, a_log, dt_bias, d_skip, ssm_norm, attn_sink, w_out, w_ff1, w_ff2):
    b, l, d = x.shape
    lc = ctx.shape[1]
    depth = w_mod.shape[0]
    assert b <= 7 and l % 512 == 0 and lc % SSM_CHUNK == 0

    cond = jnp.zeros((8, d), F32).at[:b].set(c).at[b].set(c_ctx)
    mods = _modulation(cond, w_mod, b_mod)
    cos, sin = _rope_tables(l)
    cos_c, sin_c = cos[:lc], sin[:lc]
    h_zero = jnp.zeros((b, SSM_STATE, D_SSM), F32)

    dt0 = D_XBC + D_SSM
    dt1 = dt0 + 2 * SSM_HEADS
    tm_c = min(lc, 512)
    xc = ctx
    for i in range(depth):
        last = i == depth - 1
        lat = [m.reshape(b, 1, d) for m in jnp.split(mods[i, :b], N_MOD, axis=-1)]
        cxm = [jnp.broadcast_to(m.reshape(1, 1, d), (b, 1, d)) for m in jnp.split(mods[i, b], N_MOD, axis=-1)]
        sh_m, sc_m, g_m, sh_f, sc_f, g_f = lat
        csh_m, csc_m, cg_m, csh_f, csc_f, cg_f = cxm

        w_main = jnp.concatenate([w_in[i, :, :dt0], w_in[i, :, dt1:]], axis=1).astype(BF16)
        w_dt = jnp.pad(w_in[i, :, dt0:dt1], ((0, 0), (0, LANES - (dt1 - dt0)))).astype(BF16)
        wo = w_out[i].astype(BF16)
        w1 = w_ff1[i].astype(BF16)
        w2 = w_ff2[i].astype(BF16)
        row = lambda v: v.reshape(1, -1).astype(F32)
        bias_row = _lane_row(dt_bias[i])
        alog_row = _lane_row(a_log[i])
        sink_row = _lane_row(attn_sink[i])
        dskip_row = row(jnp.repeat(d_skip[i], SSM_HEAD_DIM))
        conv_bias = row(conv_b[i])

        xbc_c, z_c, q_c, k_c, v_c, dt_c = _inproj(xc, row(g_pre_mix[i]), csc_m, csh_m, w_main, w_dt,
                                                  cos_c, sin_c, rope=False, tm=tm_c)
        xbc_c = _conv_silu(xbc_c, conv_w[i], conv_bias, t=tm_c)
        yf_c, h_f = _ssd(xbc_c, dt_c, bias_row, alog_row, h_zero, rev=False)
        yb_c, h_b = _ssd(xbc_c, dt_c, bias_row, alog_row, h_zero, rev=True)

        xbc_l, z_l, q_l, k_l, v_l, dt_l = _inproj(x, row(g_pre_mix[i]), sc_m, sh_m, w_main, w_dt,
                                                  cos, sin, rope=True, tm=512)
        xbc_l = _conv_silu(xbc_l, conv_w[i], conv_bias, t=512)
        yf, _ = _ssd(xbc_l, dt_l, bias_row, alog_row, h_f, rev=False)
        yb, _ = _ssd(xbc_l, dt_l, bias_row, alog_row, h_b, rev=True)
        ya = _attention(q_l, k_l, v_l, k_c, v_c, sink_row, band=True)
        x, hm = _outproj(yf, yb, xbc_l, z_l, ya, x, wo, dskip_row, row(ssm_norm[i]), row(g_post_mix[i]),
                         g_m, row(g_pre_mlp[i]), sc_f, sh_f, tm=512)
        x = _mlp(hm, x, w1, w2, row(g_post_mlp[i]), g_f, tm=512, tf=512)

        if not last:
            ya_c = _attention(q_c, None, None, k_c, v_c, sink_row, band=False)
            xc, hm_c = _outproj(yf_c, yb_c, xbc_c, z_c, ya_c, xc, wo, dskip_row, row(ssm_norm[i]),
                                row(g_post_mix[i]), cg_m, row(g_pre_mlp[i]), csc_f, csh_f, tm=tm_c)
            xc = _mlp(hm_c, xc, w1, w2, row(g_post_mlp[i]), cg_f, tm=tm_c, tf=512)
    return x
```

```python
import functools

import jax
import jax.numpy as jnp
from jax import lax
from jax.experimental import pallas as pl
from jax.experimental.pallas import tpu as pltpu

F32 = jnp.float32
BF16 = jnp.bfloat16

N_MOD = 6
SSM_HEAD_DIM = 64
SSM_HEADS = 16
SSM_STATE = 128
SSM_GROUPS = 2
SSM_CONV = 5
SSM_CHUNK = 128
ATTN_HEAD_DIM = 128
ATTN_HEADS = 8
ATTN_KV_HEADS = 2
ATTN_BLOCK = 128
GRID_W = 64
ROPE_BASE = 10000.0
EPS = 1e-6

D_SSM = SSM_HEADS * SSM_HEAD_DIM
SSM_BC = SSM_GROUPS * SSM_STATE
D_XBC = D_SSM + 2 * SSM_BC
D_ATTN = ATTN_HEADS * ATTN_HEAD_DIM
KV_DIM = ATTN_KV_HEADS * ATTN_HEAD_DIM
D_MAIN = D_XBC + D_SSM + D_ATTN + 2 * KV_DIM
HEADS_PER_GROUP = SSM_HEADS // SSM_GROUPS
GROUP_COLS = HEADS_PER_GROUP * SSM_HEAD_DIM

LANES = 128
BF16_SUBLANES = 16
VMEM_LIMIT_BYTES = 56 * 1024 * 1024


def _cparams(sem, flags=None):
    return pltpu.CompilerParams(dimension_semantics=sem, vmem_limit_bytes=VMEM_LIMIT_BYTES, flags=flags)


def _rms(xf, g):
    ms = jnp.mean(xf * xf, axis=-1, keepdims=True)
    return xf * lax.rsqrt(ms + EPS) * g


def _silu(v):
    return v * jax.nn.sigmoid(v)


def _resident(shape):
    nd = len(shape)
    return pl.BlockSpec(shape, lambda *_: (0,) * nd, pipeline_mode=pl.Buffered(1))


def _mod_kernel(c_ref, w_ref, b_ref, o_ref):
    s = _silu(c_ref[...]).astype(BF16)
    o_ref[0] = jnp.dot(s, w_ref[0].astype(BF16), preferred_element_type=F32) + b_ref[0]


def _modulation(cond, w_mod, b_mod):
    depth, d, n = w_mod.shape
    tn = 1024
    return pl.pallas_call(
        _mod_kernel,
        grid=(depth, n // tn),
        in_specs=[pl.BlockSpec((8, d), lambda l, j: (0, 0)),
                  pl.BlockSpec((1, d, tn), lambda l, j: (l, 0, j)),
                  pl.BlockSpec((1, 1, tn), lambda l, j: (l, 0, j))],
        out_specs=pl.BlockSpec((1, 8, tn), lambda l, j: (l, 0, j)),
        out_shape=jax.ShapeDtypeStruct((depth, 8, n), F32),
        compiler_params=_cparams(("arbitrary", "arbitrary")),
        name="modulation",
    )(cond, w_mod, b_mod.reshape(depth, 1, n))


def _rope(v, cos, sin):
    lane = lax.broadcasted_iota(jnp.int32, v.shape, 1)
    partner = jnp.where((lane % 64) < 32, pltpu.roll(v, 96, 1), pltpu.roll(v, 32, 1))
    return v * cos + partner * sin


def _inproj_kernel(x_ref, g_ref, sc_ref, sh_ref, w_ref, wdt_ref, cos_ref, sin_ref,
                   xbc_ref, z_ref, q_ref, kv_ref, dt_ref, h_scr, *, rope):
    h = _rms(x_ref[0], g_ref[...]) * (1.0 + sc_ref[0]) + sh_ref[0]
    h_scr[...] = h.astype(BF16)

    def proj(c0, n):
        return jnp.dot(h_scr[...], w_ref[:, c0:c0 + n], preferred_element_type=F32)

    seg = 512
    for s in range(D_XBC // seg):
        xbc_ref[0, :, s * seg:(s + 1) * seg] = proj(s * seg, seg).astype(BF16)
    for s in range(D_SSM // seg):
        z_ref[0, :, s * seg:(s + 1) * seg] = proj(D_XBC + s * seg, seg).astype(BF16)

    scale = ATTN_HEAD_DIM ** -0.5
    hd = ATTN_HEAD_DIM
    q0 = D_XBC + D_SSM
    for s in range(D_ATTN // seg):
        acc = proj(q0 + s * seg, seg)
        for hh in range(seg // hd):
            v = acc[:, hh * hd:(hh + 1) * hd]
            if rope:
                v = _rope(v, cos_ref[...], sin_ref[...])
            q_ref[0, :, s * seg + hh * hd:s * seg + (hh + 1) * hd] = (v * scale).astype(BF16)
    acc = proj(q0 + D_ATTN, KV_DIM)
    for hh in range(ATTN_KV_HEADS):
        v = acc[:, hh * hd:(hh + 1) * hd]
        if rope:
            v = _rope(v, cos_ref[...], sin_ref[...])
        kv_ref[0, :, hh * hd:(hh + 1) * hd] = v.astype(BF16)
    kv_ref[0, :, KV_DIM:] = proj(q0 + D_ATTN + KV_DIM, KV_DIM).astype(BF16)
    dt_ref[0] = jnp.dot(h_scr[...], wdt_ref[...], preferred_element_type=F32)


def _inproj(x, g, sc, sh, w_main, w_dt, cos, sin, *, rope, tm):
    b, l, d = x.shape
    row = lambda w: pl.BlockSpec((1, tm, w), lambda bi, i: (bi, i, 0))
    vec = pl.BlockSpec((1, 1, d), lambda bi, i: (bi, 0, 0))
    tab = pl.BlockSpec((tm, LANES), lambda bi, i: (i, 0))
    shapes = [(D_XBC, BF16), (D_SSM, BF16), (D_ATTN, BF16), (2 * KV_DIM, BF16), (LANES, F32)]
    return pl.pallas_call(
        functools.partial(_inproj_kernel, rope=rope),
        grid=(b, l // tm),
        in_specs=[row(d), _resident((1, d)), vec, vec, _resident((d, D_MAIN)), _resident((d, LANES)), tab, tab],
        out_specs=[row(w) for w, _ in shapes],
        out_shape=[jax.ShapeDtypeStruct((b, l, w), dt) for w, dt in shapes],
        scratch_shapes=[pltpu.VMEM((tm, d), BF16)],
        compiler_params=_cparams(("arbitrary", "arbitrary")),
        name="inproj_rope" if rope else "inproj",
    )(x, g, sc, sh, w_main, w_dt, cos, sin)


def _conv_kernel(u_ref, prev_ref, next_ref, w_ref, b_ref, o_ref, ext_scr, *, t):
    i = pl.program_id(1)
    n = pl.num_programs(1)
    halo = SSM_CONV // 2
    pad = 8
    h16 = BF16_SUBLANES
    prev = prev_ref[0, h16 - pad:h16, :].astype(F32)
    nxt = next_ref[0, 0:pad, :].astype(F32)
    ext_scr[0:pad, :] = jnp.where(i > 0, prev, 0.0)
    ext_scr[pad:pad + t, :] = u_ref[0].astype(F32)
    ext_scr[pad + t:pad + t + pad, :] = jnp.where(i < n - 1, nxt, 0.0)
    acc = b_ref[...] + w_ref[0:1, :] * ext_scr[pad - halo:pad - halo + t, :]
    for k in range(1, SSM_CONV):
        acc = acc + w_ref[k:k + 1, :] * ext_scr[pad - halo + k:pad - halo + k + t, :]
    o_ref[0] = _silu(acc).astype(BF16)


def _conv_silu(u, w, bias, *, t):
    b, l, c = u.shape
    h16 = BF16_SUBLANES
    nh = l // h16
    per = t // h16
    return pl.pallas_call(
        functools.partial(_conv_kernel, t=t),
        grid=(b, l // t),
        in_specs=[pl.BlockSpec((1, t, c), lambda bi, i: (bi, i, 0)),
                  pl.BlockSpec((1, h16, c), lambda bi, i: (bi, jnp.maximum(i * per - 1, 0), 0)),
                  pl.BlockSpec((1, h16, c), lambda bi, i: (bi, jnp.minimum((i + 1) * per, nh - 1), 0)),
                  _resident((SSM_CONV, c)), _resident((1, c))],
        out_specs=pl.BlockSpec((1, t, c), lambda bi, i: (bi, i, 0)),
        out_shape=jax.ShapeDtypeStruct((b, l, c), BF16),
        scratch_shapes=[pltpu.VMEM((t + 16, c), F32)],
        compiler_params=_cparams(("arbitrary", "arbitrary")),
        name="conv_silu",
    )(u, u, u, w, bias)


def _split3(v):
    hi = v.astype(BF16)
    r1 = v - hi.astype(F32)
    mid = r1.astype(BF16)
    lo = (r1 - mid.astype(F32)).astype(BF16)
    return hi, mid, lo


def _ssd_kernel(xf_ref, xb_ref, dtf_ref, dtb_ref, bias_ref, alog_ref, h0f_ref, h0b_ref,
                yf_ref, yb_ref, hff_ref, hfb_ref, ef_scr, eb_scr):
    @pl.when(pl.program_id(1) == 0)
    def _():
        r = lax.broadcasted_iota(jnp.int32, ef_scr.shape, 0)
        col = lax.broadcasted_iota(jnp.int32, ef_scr.shape, 1)
        for h0_ref, hf_ref, e_scr, off in ((h0f_ref, hff_ref, ef_scr, 0), (h0b_ref, hfb_ref, eb_scr, SSM_HEADS)):
            hf_ref[0] = h0_ref[0]
            e_scr[...] = (r == col // SSM_HEAD_DIM + off).astype(BF16)

    _ssd_chunk(xf_ref, dtf_ref, bias_ref, alog_ref, yf_ref, hff_ref, ef_scr, rev=False)
    _ssd_chunk(xb_ref, dtb_ref, bias_ref, alog_ref, yb_ref, hfb_ref, eb_scr, rev=True)


def _ssd_chunk(xbc_ref, dt_ref, bias_ref, alog_ref, y_ref, hf_ref, e_scr, *, rev):
    q = SSM_CHUNK
    off = SSM_HEADS if rev else 0

    x = dt_ref[0] + bias_ref[...]
    dt = jnp.maximum(x, 0.0) + jnp.log1p(jnp.exp(-jnp.abs(x)))
    a = -jnp.exp(alog_ref[...])
    dta = dt * a

    ri = lax.broadcasted_iota(jnp.int32, (q, q), 0)
    ci = lax.broadcasted_iota(jnp.int32, (q, q), 1)
    mask = (ci >= ri) if rev else (ci <= ri)
    tri = mask.astype(BF16)
    cs = sum(jnp.dot(tri, part, preferred_element_type=F32) for part in _split3(dta))
    tot = cs[0:1] if rev else cs[q - 1:q]
    cs_t = cs.T
    dt_t = dt.T

    ecs = jnp.exp(cs)
    ecs_hi = ecs.astype(BF16)
    ecs_lo = (ecs - ecs_hi.astype(F32)).astype(BF16)
    w_end = (dt * jnp.exp(tot - cs)).astype(BF16)
    exp_all = jnp.dot(jnp.concatenate([w_end, ecs_hi, ecs_lo], axis=0), e_scr[...],
                      preferred_element_type=F32)
    wx = exp_all[0:q]
    ecsx = exp_all[q:2 * q] + exp_all[2 * q:3 * q]
    cdec = ecsx[0:1] if rev else ecsx[q - 1:q]

    lane_lo = lax.broadcasted_iota(jnp.int32, (q, LANES), 1) < SSM_HEAD_DIM
    nt = (((1,), (1,)), ((), ()))
    for g in range(SSM_GROUPS):
        gc = slice(g * GROUP_COLS, (g + 1) * GROUP_COLS)
        bg = xbc_ref[0, :, D_SSM + g * SSM_STATE:D_SSM + (g + 1) * SSM_STATE]
        cg = xbc_ref[0, :, D_SSM + SSM_BC + g * SSM_STATE:D_SSM + SSM_BC + (g + 1) * SSM_STATE]
        cb = lax.dot_general(cg, bg, nt, preferred_element_type=F32)
        s_in = hf_ref[0, :, gc]
        y_off = jnp.dot(cg, s_in.astype(BF16), preferred_element_type=F32) * ecsx[:, gc]
        xg = xbc_ref[0, :, gc]
        xw = (xg.astype(F32) * wx[:, gc]).astype(BF16)
        bg_t = bg.astype(F32).T.astype(BF16)
        hf_ref[0, :, gc] = s_in * cdec[:, gc] + jnp.dot(bg_t, xw, preferred_element_type=F32)

        for jp in range(HEADS_PER_GROUP // 2):
            pair = g * (HEADS_PER_GROUP // 2) + jp

            def head_mat(h):
                seg = cs[:, off + h:off + h + 1] - cs_t[off + h:off + h + 1, :]
                dec = jnp.exp(jnp.where(mask, seg, -jnp.inf))
                return (cb * dec * dt_t[off + h:off + h + 1, :]).astype(BF16)

            lhs = jnp.concatenate([head_mat(2 * pair), head_mat(2 * pair + 1)], axis=1)
            xp = xbc_ref[0, :, pair * LANES:(pair + 1) * LANES]
            zero = jnp.zeros_like(xp)
            rhs = jnp.concatenate([jnp.where(lane_lo, xp, zero), jnp.where(lane_lo, zero, xp)], axis=0)
            y_d = jnp.dot(lhs, rhs, preferred_element_type=F32)
            y_ref[0, :, pair * LANES:(pair + 1) * LANES] = (
                y_d + y_off[:, jp * LANES:(jp + 1) * LANES]).astype(BF16)


def _ssd(xbc, dt_raw, dt_bias_row, a_log_row, h0_f, h0_b):
    b, l, _ = xbc.shape
    q = SSM_CHUNK
    nc = l // q
    fwd = lambda w: pl.BlockSpec((1, q, w), lambda bi, c: (bi, c, 0))
    bwd = lambda w: pl.BlockSpec((1, q, w), lambda bi, c: (bi, nc - 1 - c, 0))
    st = pl.BlockSpec((1, SSM_STATE, D_SSM), lambda bi, c: (bi, 0, 0))
    y_shape = jax.ShapeDtypeStruct((b, l, D_SSM), BF16)
    h_shape = jax.ShapeDtypeStruct((b, SSM_STATE, D_SSM), F32)
    return pl.pallas_call(
        _ssd_kernel,
        grid=(b, nc),
        in_specs=[fwd(D_XBC), bwd(D_XBC), fwd(LANES), bwd(LANES),
                  _resident((1, LANES)), _resident((1, LANES)), st, st],
        out_specs=[fwd(D_SSM), bwd(D_SSM), st, st],
        out_shape=[y_shape, y_shape, h_shape, h_shape],
        scratch_shapes=[pltpu.VMEM((LANES, D_SSM), BF16), pltpu.VMEM((LANES, D_SSM), BF16)],
        compiler_params=_cparams(("arbitrary", "arbitrary")),
        name="ssd",
    )(xbc, xbc, dt_raw, dt_raw, dt_bias_row, a_log_row, h0_f, h0_b)


def _attn_kernel(*refs, band):
    if band:
        q_ref, kvp_ref, kvc_ref, kvn_ref, kvx_ref, sink_ref, o_ref = refs
    else:
        q_ref, kvx_ref, sink_ref, o_ref = refs
    i = pl.program_id(1)
    t = ATTN_BLOCK
    hd = ATTN_HEAD_DIM
    rep = ATTN_HEADS // ATTN_KV_HEADS
    tq = q_ref.shape[1]
    nsub = tq // t if band else 1
    rows_per = t if band else tq
    nblk = pl.num_programs(1) * nsub
    lc = kvx_ref.shape[1]
    nk = (3 * t if band else 0) + lc
    nt = (((1,), (1,)), ((), ()))

    if band:
        rpos = lax.broadcasted_iota(jnp.int32, (rep * t, nk), 0) % t
        col = lax.broadcasted_iota(jnp.int32, (rep * t, nk), 1)
    rblk = lax.broadcasted_iota(jnp.int32, (rep * rows_per, 1), 0) // rows_per

    for g in range(ATTN_KV_HEADS):
        ks = slice(g * hd, (g + 1) * hd)
        vs = slice(KV_DIM + g * hd, KV_DIM + (g + 1) * hd)
        sink = jnp.zeros((rep * rows_per, 1), F32)
        for r in range(rep):
            sink = jnp.where(rblk == r, sink_ref[:, g * rep + r:g * rep + r + 1], sink)
        if band:
            band_k = jnp.concatenate([kvp_ref[0, :, ks], kvc_ref[0, :, ks], kvn_ref[0, :, ks]], axis=0)
            band_v = jnp.concatenate([kvp_ref[0, :, vs], kvc_ref[0, :, vs], kvn_ref[0, :, vs]], axis=0)
        for j in range(nsub):
            rows = slice(j * rows_per, (j + 1) * rows_per)
            qg = jnp.concatenate([q_ref[0, rows, (g * rep + r) * hd:(g * rep + r + 1) * hd] for r in range(rep)],
                                 axis=0)
            if band:
                k_all = jnp.concatenate([band_k[j * t:(j + 3) * t], kvx_ref[0, :, ks]], axis=0)
                v_all = jnp.concatenate([band_v[j * t:(j + 3) * t], kvx_ref[0, :, vs]], axis=0)
            else:
                k_all = kvx_ref[0, :, ks]
                v_all = kvx_ref[0, :, vs]
            s = lax.dot_general(qg, k_all, nt, preferred_element_type=F32)
            if band:
                qb = i * nsub + j
                lo = rpos + jnp.where(qb > 0, 0, 2 * t)
                hi = rpos + 2 * t - jnp.where(qb < nblk - 1, 0, 2 * t)
                valid = ~(((col < t) & (col < lo)) | ((col >= 2 * t) & (col < 3 * t) & (col > hi)))
                s = jnp.where(valid, s, -jnp.inf)
            m = jnp.maximum(jnp.max(s, axis=-1, keepdims=True), sink)
            p = jnp.exp(s - m)
            denom = jnp.sum(p, axis=-1, keepdims=True) + jnp.exp(sink - m)
            o = jnp.dot(p.astype(BF16), v_all, preferred_element_type=F32) / denom
            for r in range(rep):
                o_ref[0, rows, (g * rep + r) * hd:(g * rep + r + 1) * hd] = (
                    o[r * rows_per:(r + 1) * rows_per].astype(BF16))


def _attention(q, kv, kvx, sink_row, *, band):
    b, l, _ = q.shape
    t = ATTN_BLOCK
    tq = min(l, 2 * t)
    nsub = tq // t
    nblk = l // t
    lc = kvx.shape[1]
    qs = pl.BlockSpec((1, tq, D_ATTN), lambda bi, i: (bi, i, 0))
    ctx = pl.BlockSpec((1, lc, 2 * KV_DIM), lambda bi, i: (bi, 0, 0))
    sk = _resident((1, LANES))
    if band:
        prv = pl.BlockSpec((1, t, 2 * KV_DIM), lambda bi, i: (bi, jnp.maximum(i * nsub - 1, 0), 0))
        cur = pl.BlockSpec((1, tq, 2 * KV_DIM), lambda bi, i: (bi, i, 0))
        nxt = pl.BlockSpec((1, t, 2 * KV_DIM), lambda bi, i: (bi, jnp.minimum((i + 1) * nsub, nblk - 1), 0))
        in_specs = [qs, prv, cur, nxt, ctx, sk]
        args = (q, kv, kv, kv, kvx, sink_row)
    else:
        in_specs = [qs, ctx, sk]
        args = (q, kvx, sink_row)
    return pl.pallas_call(
        functools.partial(_attn_kernel, band=band),
        grid=(b, l // tq),
        in_specs=in_specs,
        out_specs=qs,
        out_shape=jax.ShapeDtypeStruct((b, l, D_ATTN), BF16),
        compiler_params=_cparams(("arbitrary", "arbitrary")),
        name="attn_band" if band else "attn_ctx",
    )(*args)


def _outproj_kernel(yf_ref, yb_ref, xs_ref, z_ref, ya_ref, x_ref, w_ref, dsk_ref, gn_ref, gpost_ref,
                    gate_ref, gpre_ref, scf_ref, shf_ref, xo_ref, hm_ref, *, sub):
    nsub = x_ref.shape[1] // sub
    blocks = [slice(r * sub, (r + 1) * sub) for r in range(nsub)]
    ys = []
    for rows in blocks:
        y = (yf_ref[0, rows, :].astype(F32) + yb_ref[0, rows, :].astype(F32)
             + xs_ref[0, rows, :].astype(F32) * dsk_ref[...])
        y = y * _silu(z_ref[0, rows, :].astype(F32))
        ys.append(_rms(y, gn_ref[...]).astype(BF16))
    mixes = [jnp.dot(y, w_ref[0:D_SSM, :], preferred_element_type=F32)
             + jnp.dot(ya_ref[0, rows, :], w_ref[D_SSM:, :], preferred_element_type=F32)
             for y, rows in zip(ys, blocks)]
    for mix, rows in zip(mixes, blocks):
        xn = x_ref[0, rows, :] + gate_ref[0] * _rms(mix, gpost_ref[...])
        xo_ref[0, rows, :] = xn
        hm_ref[0, rows, :] = (_rms(xn, gpre_ref[...]) * (1.0 + scf_ref[0]) + shf_ref[0]).astype(BF16)


def _outproj(yf, yb, xbc, z, ya, x, w_out, dskip_row, ssm_norm, g_post, gate, g_pre, sc_f, sh_f, *, tm):
    b, l, d = x.shape
    row = lambda w: pl.BlockSpec((1, tm, w), lambda bi, i: (bi, i, 0))
    vec = pl.BlockSpec((1, 1, d), lambda bi, i: (bi, 0, 0))
    return pl.pallas_call(
        functools.partial(_outproj_kernel, sub=min(tm, 256)),
        grid=(b, l // tm),
        in_specs=[row(D_SSM), row(D_SSM), row(D_SSM), row(D_SSM), row(D_ATTN), row(d),
                  _resident(w_out.shape), _resident((1, D_SSM)), _resident((1, D_SSM)), _resident((1, d)),
                  vec, _resident((1, d)), vec, vec],
        out_specs=[row(d), row(d)],
        out_shape=[jax.ShapeDtypeStruct((b, l, d), F32), jax.ShapeDtypeStruct((b, l, d), BF16)],
        compiler_params=_cparams(("arbitrary", "arbitrary")),
        name="outproj",
    )(yf, yb, xbc, z, ya, x, w_out, dskip_row, ssm_norm, g_post, gate, g_pre, sc_f, sh_f)


def _mlp_kernel(h_ref, x_ref, w1_ref, w2_ref, g_ref, gate_ref, xo_ref, acc_ref):
    k = pl.program_id(2)

    @pl.when(k == 0)
    def _():
        acc_ref[...] = jnp.zeros_like(acc_ref)

    a = jnp.dot(h_ref[0], w1_ref[...], preferred_element_type=F32)
    a = jnp.square(jnp.maximum(a, 0.0)).astype(BF16)
    acc_ref[...] += jnp.dot(a, w2_ref[...], preferred_element_type=F32)

    @pl.when(k == pl.num_programs(2) - 1)
    def _():
        xo_ref[0] = x_ref[0] + gate_ref[0] * _rms(acc_ref[...], g_ref[...])


def _mlp(h, x, w1, w2, g_post, gate, *, tm, tf):
    b, l, d = x.shape
    dff = w1.shape[1]
    row = pl.BlockSpec((1, tm, d), lambda bi, i, k: (bi, i, 0))
    return pl.pallas_call(
        _mlp_kernel,
        grid=(b, l // tm, dff // tf),
        in_specs=[row, row,
                  pl.BlockSpec((d, tf), lambda bi, i, k: (0, k)),
                  pl.BlockSpec((tf, d), lambda bi, i, k: (k, 0)),
                  _resident((1, d)),
                  pl.BlockSpec((1, 1, d), lambda bi, i, k: (bi, 0, 0))],
        out_specs=row,
        out_shape=jax.ShapeDtypeStruct((b, l, d), F32),
        scratch_shapes=[pltpu.VMEM((tm, d), F32)],
        compiler_params=_cparams(("arbitrary", "arbitrary", "arbitrary")),
        name="mlp",
    )(h, x, w1, w2, g_post, gate)


def _rope_tables(l):
    n_freq = ATTN_HEAD_DIM // 4
    inv_freq = ROPE_BASE ** (-jnp.arange(n_freq, dtype=F32) / n_freq)
    t = jnp.arange(l, dtype=jnp.int32)
    ang_r = (t // GRID_W).astype(F32)[:, None] * inv_freq[None, :]
    ang_c = (t % GRID_W).astype(F32)[:, None] * inv_freq[None, :]
    cos = jnp.concatenate([jnp.cos(ang_r)] * 2 + [jnp.cos(ang_c)] * 2, axis=-1)
    sin = jnp.concatenate([-jnp.sin(ang_r), jnp.sin(ang_r), -jnp.sin(ang_c), jnp.sin(ang_c)], axis=-1)
    return cos, sin


def _lane_row(v):
    return jnp.pad(v.reshape(1, -1).astype(F32), ((0, 0), (0, LANES - v.size)))


def kernel(x, c, ctx, c_ctx, w_mod, b_mod, g_pre_mix, g_post_mix, g_pre_mlp, g_post_mlp, w_in, conv_w,
           conv_b, a_log, dt_bias, d_skip, ssm_norm, attn_sink, w_out, w_ff1, w_ff2):
    b, l, d = x.shape
    lc = ctx.shape[1]
    depth = w_mod.shape[0]
    assert b <= 7 and l % 512 == 0 and lc % SSM_CHUNK == 0

    cond = jnp.zeros((8, d), F32).at[:b].set(c).at[b].set(c_ctx)
    mods = _modulation(cond, w_mod, b_mod)
    cos, sin = _rope_tables(l)
    cos_c, sin_c = cos[:lc], sin[:lc]
    h_zero = jnp.zeros((b, SSM_STATE, D_SSM), F32)

    dt0 = D_XBC + D_SSM
    dt1 = dt0 + 2 * SSM_HEADS
    tm_c = min(lc, 512)
    xc = ctx
    for i in range(depth):
        last = i == depth - 1
        lat = [m.reshape(b, 1, d) for m in jnp.split(mods[i, :b], N_MOD, axis=-1)]
        cxm = [jnp.broadcast_to(m.reshape(1, 1, d), (b, 1, d)) for m in jnp.split(mods[i, b], N_MOD, axis=-1)]
        sh_m, sc_m, g_m, sh_f, sc_f, g_f = lat
        csh_m, csc_m, cg_m, csh_f, csc_f, cg_f = cxm

        w_main = jnp.concatenate([w_in[i, :, :dt0], w_in[i, :, dt1:]], axis=1).astype(BF16)
        w_dt = jnp.pad(w_in[i, :, dt0:dt1], ((0, 0), (0, LANES - (dt1 - dt0)))).astype(BF16)
        wo = w_out[i].astype(BF16)
        w1 = w_ff1[i].astype(BF16)
        w2 = w_ff2[i].astype(BF16)
        row = lambda v: v.reshape(1, -1).astype(F32)
        bias_row = _lane_row(dt_bias[i])
        alog_row = _lane_row(a_log[i])
        sink_row = _lane_row(attn_sink[i])
        dskip_row = row(jnp.repeat(d_skip[i], SSM_HEAD_DIM))
        conv_bias = row(conv_b[i])

        xbc_c, z_c, q_c, kv_c, dt_c = _inproj(xc, row(g_pre_mix[i]), csc_m, csh_m, w_main, w_dt,
                                              cos_c, sin_c, rope=False, tm=tm_c)
        xbc_c = _conv_silu(xbc_c, conv_w[i], conv_bias, t=tm_c)
        yf_c, yb_c, h_f, h_b = _ssd(xbc_c, dt_c, bias_row, alog_row, h_zero, h_zero)

        xbc_l, z_l, q_l, kv_l, dt_l = _inproj(x, row(g_pre_mix[i]), sc_m, sh_m, w_main, w_dt,
                                              cos, sin, rope=True, tm=512)
        xbc_l = _conv_silu(xbc_l, conv_w[i], conv_bias, t=512)
        yf, yb, _, _ = _ssd(xbc_l, dt_l, bias_row, alog_row, h_f, h_b)
        ya = _attention(q_l, kv_l, kv_c, sink_row, band=True)
        x, hm = _outproj(yf, yb, xbc_l, z_l, ya, x, wo, dskip_row, row(ssm_norm[i]), row(g_post_mix[i]),
                         g_m, row(g_pre_mlp[i]), sc_f, sh_f, tm=512)
        x = _mlp(hm, x, w1, w2, row(g_post_mlp[i]), g_f, tm=512, tf=1024)

        if not last:
            ya_c = _attention(q_c, None, kv_c, sink_row, band=False)
            xc, hm_c = _outproj(yf_c, yb_c, xbc_c, z_c, ya_c, xc, wo, dskip_row, row(ssm_norm[i]),
                                row(g_post_mix[i]), cg_m, row(g_pre_mlp[i]), csc_f, csh_f, tm=tm_c)
            xc = _mlp(hm_c, xc, w1, w2, row(g_post_mlp[i]), cg_f, tm=tm_c, tf=1024)
    return x
```

```python
import functools

import jax
import jax.numpy as jnp
from jax import lax
from jax.experimental import pallas as pl
from jax.experimental.pallas import tpu as pltpu

F32 = jnp.float32
BF16 = jnp.bfloat16

N_MOD = 6
SSM_HEAD_DIM = 64
SSM_HEADS = 16
SSM_STATE = 128
SSM_GROUPS = 2
SSM_CONV = 5
SSM_CHUNK = 128
ATTN_HEAD_DIM = 128
ATTN_HEADS = 8
ATTN_KV_HEADS = 2
ATTN_BLOCK = 128
CONV_ROWS = 128
GRID_W = 64
ROPE_BASE = 10000.0
EPS = 1e-6
LOG2_E = 1.4426950408889634

D_SSM = SSM_HEADS * SSM_HEAD_DIM
SSM_BC = SSM_GROUPS * SSM_STATE
D_XBC = D_SSM + 2 * SSM_BC
D_ATTN = ATTN_HEADS * ATTN_HEAD_DIM
KV_DIM = ATTN_KV_HEADS * ATTN_HEAD_DIM
D_MAIN = D_XBC + D_SSM + D_ATTN + KV_DIM
HEADS_PER_GROUP = SSM_HEADS // SSM_GROUPS
GROUP_COLS = HEADS_PER_GROUP * SSM_HEAD_DIM

LANES = 128
BF16_SUBLANES = 16
VMEM_LIMIT_BYTES = 56 * 1024 * 1024


def _cparams(sem, flags=None):
    return pltpu.CompilerParams(dimension_semantics=sem, vmem_limit_bytes=VMEM_LIMIT_BYTES, flags=flags)


def _rms(xf, g):
    ms = jnp.mean(xf * xf, axis=-1, keepdims=True)
    return xf * lax.rsqrt(ms + EPS) * g


def _silu(v):
    return v * jax.nn.sigmoid(v)


def _resident(shape):
    nd = len(shape)
    return pl.BlockSpec(shape, lambda *_: (0,) * nd, pipeline_mode=pl.Buffered(1))


def _mod_kernel(c_ref, w_ref, b_ref, o_ref):
    s = _silu(c_ref[...]).astype(BF16)
    o_ref[0] = jnp.dot(s, w_ref[0].astype(BF16), preferred_element_type=F32) + b_ref[0]


def _modulation(cond, w_mod, b_mod):
    depth, d, n = w_mod.shape
    tn = 1024
    return pl.pallas_call(
        _mod_kernel,
        grid=(depth, n // tn),
        in_specs=[pl.BlockSpec((8, d), lambda l, j: (0, 0)),
                  pl.BlockSpec((1, d, tn), lambda l, j: (l, 0, j)),
                  pl.BlockSpec((1, 1, tn), lambda l, j: (l, 0, j))],
        out_specs=pl.BlockSpec((1, 8, tn), lambda l, j: (l, 0, j)),
        out_shape=jax.ShapeDtypeStruct((depth, 8, n), F32),
        compiler_params=_cparams(("arbitrary", "arbitrary")),
        name="modulation",
    )(cond, w_mod, b_mod.reshape(depth, 1, n))


def _rope(v, cos, sin):
    lane = lax.broadcasted_iota(jnp.int32, v.shape, 1)
    partner = jnp.where((lane % 64) < 32, pltpu.roll(v, 96, 1), pltpu.roll(v, 32, 1))
    return v * cos + partner * sin


def _inproj_kernel(x_ref, g_ref, sc_ref, sh_ref, w_ref, wvt_ref, wdt_ref, cos_ref, sin_ref,
                   xbc_ref, z_ref, q_ref, k_ref, vt_ref, dt_ref, h_scr, *, rope):
    h = _rms(x_ref[0], g_ref[...]) * (1.0 + sc_ref[0]) + sh_ref[0]
    h_scr[...] = h.astype(BF16)

    def proj(c0, n):
        return jnp.dot(h_scr[...], w_ref[:, c0:c0 + n], preferred_element_type=F32)

    seg = 512
    for s in range(D_XBC // seg):
        xbc_ref[0, :, s * seg:(s + 1) * seg] = proj(s * seg, seg).astype(BF16)
    for s in range(D_SSM // seg):
        z_ref[0, :, s * seg:(s + 1) * seg] = proj(D_XBC + s * seg, seg).astype(BF16)

    scale = ATTN_HEAD_DIM ** -0.5 * LOG2_E
    hd = ATTN_HEAD_DIM
    q0 = D_XBC + D_SSM
    for s in range(D_ATTN // seg):
        acc = proj(q0 + s * seg, seg)
        for hh in range(seg // hd):
            v = acc[:, hh * hd:(hh + 1) * hd]
            if rope:
                v = _rope(v, cos_ref[...], sin_ref[...])
            q_ref[0, :, s * seg + hh * hd:s * seg + (hh + 1) * hd] = (v * scale).astype(BF16)
    acc = proj(q0 + D_ATTN, KV_DIM)
    for hh in range(ATTN_KV_HEADS):
        v = acc[:, hh * hd:(hh + 1) * hd]
        if rope:
            v = _rope(v, cos_ref[...], sin_ref[...])
        k_ref[0, :, hh * hd:(hh + 1) * hd] = v.astype(BF16)
    vt_ref[0] = lax.dot_general(wvt_ref[...], h_scr[...], (((1,), (1,)), ((), ())),
                                preferred_element_type=F32).astype(BF16)
    dt_ref[0] = jnp.dot(h_scr[...], wdt_ref[...], preferred_element_type=F32)


def _inproj(x, g, sc, sh, w_main, w_vt, w_dt, cos, sin, *, rope, tm):
    b, l, d = x.shape
    row = lambda w: pl.BlockSpec((1, tm, w), lambda bi, i: (bi, i, 0))
    vec = pl.BlockSpec((1, 1, d), lambda bi, i: (bi, 0, 0))
    tab = pl.BlockSpec((tm, LANES), lambda bi, i: (i, 0))
    widths = [(D_XBC, BF16), (D_SSM, BF16), (D_ATTN, BF16), (KV_DIM, BF16)]
    out_specs = [row(w) for w, _ in widths] + [pl.BlockSpec((1, KV_DIM, tm), lambda bi, i: (bi, 0, i)), row(LANES)]
    out_shape = ([jax.ShapeDtypeStruct((b, l, w), dt) for w, dt in widths]
                 + [jax.ShapeDtypeStruct((b, KV_DIM, l), BF16), jax.ShapeDtypeStruct((b, l, LANES), F32)])
    return pl.pallas_call(
        functools.partial(_inproj_kernel, rope=rope),
        grid=(b, l // tm),
        in_specs=[row(d), _resident((1, d)), vec, vec, _resident((d, D_MAIN)), _resident((KV_DIM, d)),
                  _resident((d, LANES)), tab, tab],
        out_specs=out_specs,
        out_shape=out_shape,
        scratch_shapes=[pltpu.VMEM((tm, d), BF16)],
        compiler_params=_cparams(("arbitrary", "arbitrary")),
        name="inproj_rope" if rope else "inproj",
    )(x, g, sc, sh, w_main, w_vt, w_dt, cos, sin)


def _conv_kernel(u_ref, prev_ref, next_ref, w_ref, b_ref, o_ref, ext_scr, *, t):
    i = pl.program_id(1)
    n = pl.num_programs(1)
    halo = SSM_CONV // 2
    h16 = BF16_SUBLANES
    zero = jnp.zeros(prev_ref.shape[1:], BF16)
    ext_scr[0:h16, :] = jnp.where(i > 0, prev_ref[0], zero)
    ext_scr[h16:h16 + t, :] = u_ref[0]
    ext_scr[h16 + t:h16 + t + h16, :] = jnp.where(i < n - 1, next_ref[0], zero)
    blk = CONV_ROWS
    win = blk + 2 * h16
    rr = lax.broadcasted_iota(jnp.int32, (blk, win), 0)
    cc = lax.broadcasted_iota(jnp.int32, (blk, win), 1)
    shifts = [(cc == rr + h16 + k - halo).astype(BF16) for k in range(SSM_CONV)]
    for r in range(t // blk):
        window = ext_scr[r * blk:r * blk + win, :]
        acc = b_ref[...]
        for k in range(SSM_CONV):
            acc = acc + w_ref[k:k + 1, :] * jnp.dot(shifts[k], window, preferred_element_type=F32)
        o_ref[0, r * blk:(r + 1) * blk, :] = _silu(acc).astype(BF16)


def _conv_silu(u, w, bias, *, t):
    b, l, c = u.shape
    h16 = BF16_SUBLANES
    nh = l // h16
    per = t // h16
    return pl.pallas_call(
        functools.partial(_conv_kernel, t=t),
        grid=(b, l // t),
        in_specs=[pl.BlockSpec((1, t, c), lambda bi, i: (bi, i, 0)),
                  pl.BlockSpec((1, h16, c), lambda bi, i: (bi, jnp.maximum(i * per - 1, 0), 0)),
                  pl.BlockSpec((1, h16, c), lambda bi, i: (bi, jnp.minimum((i + 1) * per, nh - 1), 0)),
                  _resident((SSM_CONV, c)), _resident((1, c))],
        out_specs=pl.BlockSpec((1, t, c), lambda bi, i: (bi, i, 0)),
        out_shape=jax.ShapeDtypeStruct((b, l, c), BF16),
        scratch_shapes=[pltpu.VMEM((t + 2 * h16, c), BF16)],
        compiler_params=_cparams(("arbitrary", "arbitrary")),
        name="conv_silu",
    )(u, u, u, w, bias)


def _split3(v):
    hi = v.astype(BF16)
    r1 = v - hi.astype(F32)
    mid = r1.astype(BF16)
    lo = (r1 - mid.astype(F32)).astype(BF16)
    return hi, mid, lo


def _ssd_kernel(xf_ref, xb_ref, dtf_ref, dtb_ref, bias_ref, alog_ref, h0f_ref, h0b_ref,
                yf_ref, yb_ref, hff_ref, hfb_ref, ef_scr, eb_scr):
    @pl.when(pl.program_id(1) == 0)
    def _():
        r = lax.broadcasted_iota(jnp.int32, ef_scr.shape, 0) % LANES
        col = lax.broadcasted_iota(jnp.int32, ef_scr.shape, 1)
        for h0_ref, hf_ref, e_scr, off in ((h0f_ref, hff_ref, ef_scr, 0), (h0b_ref, hfb_ref, eb_scr, SSM_HEADS)):
            hf_ref[0] = h0_ref[0]
            e_scr[...] = (r == col // SSM_HEAD_DIM + off).astype(BF16)

    prep_f = _ssd_prep(dtf_ref, bias_ref, alog_ref, ef_scr, rev=False)
    prep_b = _ssd_prep(dtb_ref, bias_ref, alog_ref, eb_scr, rev=True)
    _ssd_chunk(xf_ref, yf_ref, hff_ref, prep_f, rev=False)
    _ssd_chunk(xb_ref, yb_ref, hfb_ref, prep_b, rev=True)


def _ssd_prep(dt_ref, bias_ref, alog_ref, e_scr, *, rev):
    q = SSM_CHUNK
    x = dt_ref[0] + bias_ref[...]
    dt = jnp.maximum(x, 0.0) + jnp.log1p(jnp.exp(-jnp.abs(x)))
    a2 = -jnp.exp(alog_ref[...]) * LOG2_E
    dta = dt * a2

    ri = lax.broadcasted_iota(jnp.int32, (q, q), 0)
    ci = lax.broadcasted_iota(jnp.int32, (q, q), 1)
    mask = (ci >= ri) if rev else (ci <= ri)
    tri = mask.astype(BF16)
    cs = sum(jnp.dot(tri, part, preferred_element_type=F32) for part in _split3(dta))
    tot = cs[0:1] if rev else cs[q - 1:q]
    row_t = (cs - jnp.log2(dt)).T

    ecs = jnp.exp2(cs)
    ecs_hi = ecs.astype(BF16)
    ecs_lo = (ecs - ecs_hi.astype(F32)).astype(BF16)
    w_end = (dt * jnp.exp2(tot - cs)).astype(BF16)
    wx = jnp.dot(w_end, e_scr[0:LANES, :], preferred_element_type=F32)
    ecsx = jnp.dot(jnp.concatenate([ecs_hi, ecs_lo], axis=1), e_scr[...], preferred_element_type=F32)
    return dict(mask=mask, cs=cs, row_t=row_t, wx=wx, ecsx=ecsx)


def _ssd_chunk(xbc_ref, y_ref, hf_ref, prep, *, rev):
    q = SSM_CHUNK
    off = SSM_HEADS if rev else 0
    mask, cs, row_t, wx, ecsx = prep["mask"], prep["cs"], prep["row_t"], prep["wx"], prep["ecsx"]
    cdec = ecsx[0:1] if rev else ecsx[q - 1:q]

    lane_lo = lax.broadcasted_iota(jnp.int32, (q, LANES), 1) < SSM_HEAD_DIM
    nt = (((1,), (1,)), ((), ()))
    for g in range(SSM_GROUPS):
        gc = slice(g * GROUP_COLS, (g + 1) * GROUP_COLS)
        bg = xbc_ref[0, :, D_SSM + g * SSM_STATE:D_SSM + (g + 1) * SSM_STATE]
        cg = xbc_ref[0, :, D_SSM + SSM_BC + g * SSM_STATE:D_SSM + SSM_BC + (g + 1) * SSM_STATE]
        cb = lax.dot_general(cg, bg, nt, preferred_element_type=F32)
        s_in = hf_ref[0, :, gc]
        y_off = jnp.dot(cg, s_in.astype(BF16), preferred_element_type=F32) * ecsx[:, gc]
        xg = xbc_ref[0, :, gc]
        xw = (xg.astype(F32) * wx[:, gc]).astype(BF16)
        bg_t = bg.astype(F32).T.astype(BF16)
        hf_ref[0, :, gc] = s_in * cdec[:, gc] + jnp.dot(bg_t, xw, preferred_element_type=F32)

        for jp in range(HEADS_PER_GROUP // 2):
            pair = g * (HEADS_PER_GROUP // 2) + jp

            def head_mat(h):
                seg = cs[:, off + h:off + h + 1] - row_t[off + h:off + h + 1, :]
                return (cb * jnp.exp2(jnp.where(mask, seg, -jnp.inf))).astype(BF16)

            lhs = jnp.concatenate([head_mat(2 * pair), head_mat(2 * pair + 1)], axis=1)
            xp = xbc_ref[0, :, pair * LANES:(pair + 1) * LANES]
            zero = jnp.zeros_like(xp)
            rhs = jnp.concatenate([jnp.where(lane_lo, xp, zero), jnp.where(lane_lo, zero, xp)], axis=0)
            y_d = jnp.dot(lhs, rhs, preferred_element_type=F32)
            y_ref[0, :, pair * LANES:(pair + 1) * LANES] = (
                y_d + y_off[:, jp * LANES:(jp + 1) * LANES]).astype(BF16)


def _ssd(xbc, dt_raw, dt_bias_row, a_log_row, h0_f, h0_b):
    b, l, _ = xbc.shape
    q = SSM_CHUNK
    nc = l // q
    fwd = lambda w: pl.BlockSpec((1, q, w), lambda bi, c: (bi, c, 0))
    bwd = lambda w: pl.BlockSpec((1, q, w), lambda bi, c: (bi, nc - 1 - c, 0))
    st = pl.BlockSpec((1, SSM_STATE, D_SSM), lambda bi, c: (bi, 0, 0))
    y_shape = jax.ShapeDtypeStruct((b, l, D_SSM), BF16)
    h_shape = jax.ShapeDtypeStruct((b, SSM_STATE, D_SSM), F32)
    return pl.pallas_call(
        _ssd_kernel,
        grid=(b, nc),
        in_specs=[fwd(D_XBC), bwd(D_XBC), fwd(LANES), bwd(LANES),
                  _resident((1, LANES)), _resident((1, LANES)), st, st],
        out_specs=[fwd(D_SSM), bwd(D_SSM), st, st],
        out_shape=[y_shape, y_shape, h_shape, h_shape],
        scratch_shapes=[pltpu.VMEM((2 * LANES, D_SSM), BF16), pltpu.VMEM((2 * LANES, D_SSM), BF16)],
        compiler_params=_cparams(("arbitrary", "arbitrary")),
        name="ssd",
    )(xbc, xbc, dt_raw, dt_raw, dt_bias_row, a_log_row, h0_f, h0_b)


def _attn_kernel(*refs, band):
    if band:
        q_ref, kp_ref, kc_ref, kn_ref, vp_ref, vc_ref, vn_ref, kx_ref, vx_ref, sink_ref, o_ref = refs
    else:
        q_ref, kx_ref, vx_ref, sink_ref, o_ref = refs
    i = pl.program_id(1)
    t = ATTN_BLOCK
    hd = ATTN_HEAD_DIM
    rep = ATTN_HEADS // ATTN_KV_HEADS
    tq = q_ref.shape[1]
    nsub = tq // t if band else 1
    rows_per = t if band else tq
    nblk = pl.num_programs(1) * nsub
    lc = kx_ref.shape[1]
    nk = (3 * t if band else 0) + lc
    nq = rep * rows_per
    nt = (((1,), (1,)), ((), ()))

    if band:
        krow = lax.broadcasted_iota(jnp.int32, (t, nq), 0)
        qcol = lax.broadcasted_iota(jnp.int32, (t, nq), 1) % t
    qhead = lax.broadcasted_iota(jnp.int32, (1, nq), 1) // rows_per
    ones = jnp.ones((BF16_SUBLANES, nk), BF16)

    def scores(g, j):
        ks = slice(g * hd, (g + 1) * hd)
        rows = slice(j * rows_per, (j + 1) * rows_per)
        qg = jnp.concatenate([q_ref[0, rows, (g * rep + r) * hd:(g * rep + r + 1) * hd] for r in range(rep)],
                             axis=0)
        if band:
            band_k = jnp.concatenate([kp_ref[0, :, ks], kc_ref[0, :, ks], kn_ref[0, :, ks]], axis=0)
            k_all = jnp.concatenate([band_k[j * t:(j + 3) * t], kx_ref[0, :, ks]], axis=0)
        else:
            k_all = kx_ref[0, :, ks]
        s = lax.dot_general(k_all, qg, nt, preferred_element_type=F32)
        if band:
            qb = i * nsub + j
            lo = qcol + jnp.where(qb > 0, 0, 2 * t)
            hi = qcol - jnp.where(qb < nblk - 1, 0, 2 * t)
            s = jnp.concatenate([jnp.where(krow >= lo, s[0:t], -jnp.inf), s[t:2 * t],
                                 jnp.where(krow <= hi, s[2 * t:3 * t], -jnp.inf), s[3 * t:]], axis=0)
        return s

    def finish(g, j, s):
        ks = slice(g * hd, (g + 1) * hd)
        rows = slice(j * rows_per, (j + 1) * rows_per)
        sink = jnp.zeros((1, nq), F32)
        for r in range(rep):
            sink = jnp.where(qhead == r, sink_ref[:, g * rep + r:g * rep + r + 1] * LOG2_E, sink)
        if band:
            band_v = jnp.concatenate([vp_ref[0, ks, :], vc_ref[0, ks, :], vn_ref[0, ks, :]], axis=1)
            v_all = jnp.concatenate([band_v[:, j * t:(j + 3) * t], vx_ref[0, ks, :]], axis=1)
        else:
            v_all = vx_ref[0, ks, :]
        m = jnp.maximum(jnp.max(s, axis=0, keepdims=True), sink)
        p = jnp.exp2(s - m).astype(BF16)
        ov = jnp.dot(jnp.concatenate([v_all, ones], axis=0), p, preferred_element_type=F32)
        inv = 1.0 / (ov[hd:hd + 1] + jnp.exp2(sink - m))
        o = ov[0:hd] * inv
        for r in range(rep):
            o_ref[0, rows, (g * rep + r) * hd:(g * rep + r + 1) * hd] = (
                o[:, r * rows_per:(r + 1) * rows_per].T.astype(BF16))

    chains = [(g, j) for g in range(ATTN_KV_HEADS) for j in range(nsub)]
    s_next = scores(*chains[0])
    for n, (g, j) in enumerate(chains):
        s_cur = s_next
        if n + 1 < len(chains):
            s_next = scores(*chains[n + 1])
        finish(g, j, s_cur)


def _attention(q, k, vt, kx, vtx, sink_row, *, band):
    b, l, _ = q.shape
    t = ATTN_BLOCK
    tq = min(l, 2 * t)
    nsub = tq // t
    nblk = l // t
    lc = kx.shape[1]
    qs = pl.BlockSpec((1, tq, D_ATTN), lambda bi, i: (bi, i, 0))
    kctx = pl.BlockSpec((1, lc, KV_DIM), lambda bi, i: (bi, 0, 0))
    vctx = pl.BlockSpec((1, KV_DIM, lc), lambda bi, i: (bi, 0, 0))
    sk = _resident((1, LANES))
    if band:
        prv = lambda i: jnp.maximum(i * nsub - 1, 0)
        nxt = lambda i: jnp.minimum((i + 1) * nsub, nblk - 1)
        kspec = lambda rows, f: pl.BlockSpec((1, rows, KV_DIM), lambda bi, i: (bi, f(i), 0))
        vspec = lambda rows, f: pl.BlockSpec((1, KV_DIM, rows), lambda bi, i: (bi, 0, f(i)))
        cur = lambda i: i
        in_specs = [qs, kspec(t, prv), kspec(tq, cur), kspec(t, nxt),
                    vspec(t, prv), vspec(tq, cur), vspec(t, nxt), kctx, vctx, sk]
        args = (q, k, k, k, vt, vt, vt, kx, vtx, sink_row)
    else:
        in_specs = [qs, kctx, vctx, sk]
        args = (q, kx, vtx, sink_row)
    return pl.pallas_call(
        functools.partial(_attn_kernel, band=band),
        grid=(b, l // tq),
        in_specs=in_specs,
        out_specs=qs,
        out_shape=jax.ShapeDtypeStruct((b, l, D_ATTN), BF16),
        compiler_params=_cparams(("arbitrary", "arbitrary")),
        name="attn_band" if band else "attn_ctx",
    )(*args)


def _outproj_kernel(yf_ref, yb_ref, xs_ref, z_ref, ya_ref, x_ref, w_ref, dsk_ref, gn_ref, gpost_ref,
                    gate_ref, gpre_ref, scf_ref, shf_ref, xo_ref, hm_ref, *, sub):
    for r in range(x_ref.shape[1] // sub):
        rows = slice(r * sub, (r + 1) * sub)
        y = (yf_ref[0, rows, :].astype(F32) + yb_ref[0, rows, :].astype(F32)
             + xs_ref[0, rows, :].astype(F32) * dsk_ref[...])
        y = y * _silu(z_ref[0, rows, :].astype(F32))
        cat = jnp.concatenate([_rms(y, gn_ref[...]).astype(BF16), ya_ref[0, rows, :]], axis=1)
        mix = jnp.dot(cat, w_ref[...], preferred_element_type=F32)
        xn = x_ref[0, rows, :] + gate_ref[0] * _rms(mix, gpost_ref[...])
        xo_ref[0, rows, :] = xn
        hm_ref[0, rows, :] = (_rms(xn, gpre_ref[...]) * (1.0 + scf_ref[0]) + shf_ref[0]).astype(BF16)


def _outproj(yf, yb, xbc, z, ya, x, w_out, dskip_row, ssm_norm, g_post, gate, g_pre, sc_f, sh_f, *, tm):
    b, l, d = x.shape
    row = lambda w: pl.BlockSpec((1, tm, w), lambda bi, i: (bi, i, 0))
    vec = pl.BlockSpec((1, 1, d), lambda bi, i: (bi, 0, 0))
    return pl.pallas_call(
        functools.partial(_outproj_kernel, sub=min(tm, 256)),
        grid=(b, l // tm),
        in_specs=[row(D_SSM), row(D_SSM), row(D_SSM), row(D_SSM), row(D_ATTN), row(d),
                  _resident(w_out.shape), _resident((1, D_SSM)), _resident((1, D_SSM)), _resident((1, d)),
                  vec, _resident((1, d)), vec, vec],
        out_specs=[row(d), row(d)],
        out_shape=[jax.ShapeDtypeStruct((b, l, d), F32), jax.ShapeDtypeStruct((b, l, d), BF16)],
        compiler_params=_cparams(("arbitrary", "arbitrary")),
        name="outproj",
    )(yf, yb, xbc, z, ya, x, w_out, dskip_row, ssm_norm, g_post, gate, g_pre, sc_f, sh_f)


def _mlp_kernel(h_ref, x_ref, w1_ref, w2_ref, g_ref, gate_ref, xo_ref, acc_ref):
    k = pl.program_id(2)

    @pl.when(k == 0)
    def _():
        acc_ref[...] = jnp.zeros_like(acc_ref)

    a = jnp.dot(h_ref[0], w1_ref[...], preferred_element_type=F32)
    a = jnp.square(jnp.maximum(a, 0.0)).astype(BF16)
    acc_ref[...] += jnp.dot(a, w2_ref[...], preferred_element_type=F32)

    @pl.when(k == pl.num_programs(2) - 1)
    def _():
        xo_ref[0] = x_ref[0] + gate_ref[0] * _rms(acc_ref[...], g_ref[...])


def _mlp(h, x, w1, w2, g_post, gate, *, tm, tf):
    b, l, d = x.shape
    dff = w1.shape[1]
    row = pl.BlockSpec((1, tm, d), lambda bi, i, k: (bi, i, 0))
    return pl.pallas_call(
        _mlp_kernel,
        grid=(b, l // tm, dff // tf),
        in_specs=[row, row,
                  pl.BlockSpec((d, tf), lambda bi, i, k: (0, k)),
                  pl.BlockSpec((tf, d), lambda bi, i, k: (k, 0)),
                  _resident((1, d)),
                  pl.BlockSpec((1, 1, d), lambda bi, i, k: (bi, 0, 0))],
        out_specs=row,
        out_shape=jax.ShapeDtypeStruct((b, l, d), F32),
        scratch_shapes=[pltpu.VMEM((tm, d), F32)],
        compiler_params=_cparams(("arbitrary", "arbitrary", "arbitrary")),
        name="mlp",
    )(h, x, w1, w2, g_post, gate)


def _rope_tables(l):
    n_freq = ATTN_HEAD_DIM // 4
    inv_freq = ROPE_BASE ** (-jnp.arange(n_freq, dtype=F32) / n_freq)
    t = jnp.arange(l, dtype=jnp.int32)
    ang_r = (t // GRID_W).astype(F32)[:, None] * inv_freq[None, :]
    ang_c = (t % GRID_W).astype(F32)[:, None] * inv_freq[None, :]
    cos = jnp.concatenate([jnp.cos(ang_r)] * 2 + [jnp.cos(ang_c)] * 2, axis=-1)
    sin = jnp.concatenate([-jnp.sin(ang_r), jnp.sin(ang_r), -jnp.sin(ang_c), jnp.sin(ang_c)], axis=-1)
    return cos, sin


def _lane_row(v):
    return jnp.pad(v.reshape(1, -1).astype(F32), ((0, 0), (0, LANES - v.size)))


def kernel(x, c, ctx, c_ctx, w_mod, b_mod, g_pre_mix, g_post_mix, g_pre_mlp, g_post_mlp, w_in, conv_w,
           conv_b, a_log, dt_bias, d_skip, ssm_norm, attn_sink, w_out, w_ff1, w_ff2):
    b, l, d = x.shape
    lc = ctx.shape[1]
    depth = w_mod.shape[0]
    assert b <= 7 and l % 512 == 0 and lc % SSM_CHUNK == 0

    cond = jnp.zeros((8, d), F32).at[:b].set(c).at[b].set(c_ctx)
    mods = _modulation(cond, w_mod, b_mod)
    cos, sin = _rope_tables(l)
    cos_c, sin_c = cos[:lc], sin[:lc]
    h_zero = jnp.zeros((b, SSM_STATE, D_SSM), F32)

    dt0 = D_XBC + D_SSM
    dt1 = dt0 + 2 * SSM_HEADS
    tm_c = min(lc, 512)
    xc = ctx
    for i in range(depth):
        last = i == depth - 1
        lat = [m.reshape(b, 1, d) for m in jnp.split(mods[i, :b], N_MOD, axis=-1)]
        cxm = [jnp.broadcast_to(m.reshape(1, 1, d), (b, 1, d)) for m in jnp.split(mods[i, b], N_MOD, axis=-1)]
        sh_m, sc_m, g_m, sh_f, sc_f, g_f = lat
        csh_m, csc_m, cg_m, csh_f, csc_f, cg_f = cxm

        v0 = dt1 + D_ATTN + KV_DIM
        w_main = jnp.concatenate([w_in[i, :, :dt0], w_in[i, :, dt1:v0]], axis=1).astype(BF16)
        w_vt = w_in[i, :, v0:].T.astype(BF16)
        w_dt = jnp.pad(w_in[i, :, dt0:dt1], ((0, 0), (0, LANES - (dt1 - dt0)))).astype(BF16)
        wo = w_out[i].astype(BF16)
        w1 = w_ff1[i].astype(BF16)
        w2 = w_ff2[i].astype(BF16)
        row = lambda v: v.reshape(1, -1).astype(F32)
        bias_row = _lane_row(dt_bias[i])
        alog_row = _lane_row(a_log[i])
        sink_row = _lane_row(attn_sink[i])
        dskip_row = row(jnp.repeat(d_skip[i], SSM_HEAD_DIM))
        conv_bias = row(conv_b[i])

        xbc_c, z_c, q_c, k_c, vt_c, dt_c = _inproj(xc, row(g_pre_mix[i]), csc_m, csh_m, w_main, w_vt, w_dt,
                                                   cos_c, sin_c, rope=False, tm=tm_c)
        xbc_c = _conv_silu(xbc_c, conv_w[i], conv_bias, t=tm_c)
        yf_c, yb_c, h_f, h_b = _ssd(xbc_c, dt_c, bias_row, alog_row, h_zero, h_zero)

        xbc_l, z_l, q_l, k_l, vt_l, dt_l = _inproj(x, row(g_pre_mix[i]), sc_m, sh_m, w_main, w_vt, w_dt,
                                                   cos, sin, rope=True, tm=512)
        xbc_l = _conv_silu(xbc_l, conv_w[i], conv_bias, t=512)
        yf, yb, _, _ = _ssd(xbc_l, dt_l, bias_row, alog_row, h_f, h_b)
        ya = _attention(q_l, k_l, vt_l, k_c, vt_c, sink_row, band=True)
        x, hm = _outproj(yf, yb, xbc_l, z_l, ya, x, wo, dskip_row, row(ssm_norm[i]), row(g_post_mix[i]),
                         g_m, row(g_pre_mlp[i]), sc_f, sh_f, tm=512)
        x = _mlp(hm, x, w1, w2, row(g_post_mlp[i]), g_f, tm=512, tf=1024)

        if not last:
            ya_c = _attention(q_c, None, None, k_c, vt_c, sink_row, band=False)
            xc, hm_c = _outproj(yf_c, yb_c, xbc_c, z_c, ya_c, xc, wo, dskip_row, row(ssm_norm[i]),
                                row(g_post_mix[i]), cg_m, row(g_pre_mlp[i]), csc_f, csh_f, tm=tm_c)
            xc = _mlp(hm_c.reshape(1, b * lc, d), xc.reshape(1, b * lc, d), w1, w2, row(g_post_mlp[i]),
                      cg_f[:1], tm=min(b * lc, 512), tf=1024).reshape(b, lc, d)
    return x
```

```python
import functools

import numpy as np
import jax
import jax.numpy as jnp
from jax import lax
from jax.experimental import pallas as pl
from jax.experimental.pallas import tpu as pltpu

F32 = jnp.float32
BF16 = jnp.bfloat16

N_MOD = 6
SSM_HEAD_DIM = 64
SSM_HEADS = 16
SSM_STATE = 128
SSM_GROUPS = 2
SSM_CONV = 5
SSM_CHUNK = 128
ATTN_HEAD_DIM = 128
ATTN_HEADS = 8
ATTN_KV_HEADS = 2
ATTN_BLOCK = 128
GRID_W = 64
ROPE_BASE = 10000.0
EPS = 1e-6
LOG2_E = 1.4426950408889634

D_SSM = SSM_HEADS * SSM_HEAD_DIM
SSM_BC = SSM_GROUPS * SSM_STATE
D_XBC = D_SSM + 2 * SSM_BC
D_ATTN = ATTN_HEADS * ATTN_HEAD_DIM
KV_DIM = ATTN_KV_HEADS * ATTN_HEAD_DIM
N_DT = 2 * SSM_HEADS
D_IN_PROJ = D_XBC + D_SSM + N_DT + D_ATTN + 2 * KV_DIM
HEADS_PER_GROUP = SSM_HEADS // SSM_GROUPS
GROUP_COLS = HEADS_PER_GROUP * SSM_HEAD_DIM

LANES = 128
BF16_SUBLANES = 16
VMEM_LIMIT_BYTES = 56 * 1024 * 1024

ROW_TILE = 512
FF_TILE = 1024
CONV_ROWS = 128

V_G_PRE_MIX, V_G_POST_MIX, V_G_PRE_MLP, V_G_POST_MLP = 0, 1, 2, 3
V_SSM_NORM, V_DSKIP, V_CONV_B, V_DT_BIAS, V_A_LOG, V_SINK = 4, 5, 6, 7, 8, 9
VEC_ROWS = 16
M_SH_MIX, M_SC_MIX, M_G_MIX, M_SH_MLP, M_SC_MLP, M_G_MLP = range(N_MOD)


def _cparams(sem):
    return pltpu.CompilerParams(dimension_semantics=sem, vmem_limit_bytes=VMEM_LIMIT_BYTES)


def _unit(xf):
    ms = jnp.mean(xf * xf, axis=-1, keepdims=True)
    return xf * lax.rsqrt(ms + EPS)


def _rms(xf, g):
    return _unit(xf) * g


def _silu(v):
    return v * jax.nn.sigmoid(v)


def _const_spec(block, index):
    return pl.BlockSpec(block, lambda *_: index, pipeline_mode=pl.Buffered(1))


def _vec_spec(vecs, layer):
    return _const_spec((1,) + vecs.shape[1:], (layer, 0, 0))


def _mod_spec(mods, layer, chunk):
    d = mods.shape[-1] // N_MOD
    return _const_spec((1, mods.shape[1], d), (layer, 0, chunk))


def _vec(v_ref, row, width=None):
    return v_ref[0, row:row + 1, :] if width is None else v_ref[0, row:row + 1, 0:width]


def _cond(m_ref, cond_row):
    if cond_row is None:
        return m_ref[0, pl.ds(pl.program_id(0), 1), :]
    return m_ref[0, cond_row:cond_row + 1, :]


def _mod_kernel(c_ref, w_ref, b_ref, o_ref):
    s = _silu(c_ref[...]).astype(BF16)
    o_ref[0] = jnp.dot(s, w_ref[0].astype(BF16), preferred_element_type=F32) + b_ref[0]


def _modulation(cond, w_mod, b_mod):
    depth, d, n = w_mod.shape
    tn = 1024
    return pl.pallas_call(
        _mod_kernel,
        grid=(depth, n // tn),
        in_specs=[pl.BlockSpec((8, d), lambda l, j: (0, 0)),
                  pl.BlockSpec((1, d, tn), lambda l, j: (l, 0, j)),
                  pl.BlockSpec((1, 1, tn), lambda l, j: (l, 0, j))],
        out_specs=pl.BlockSpec((1, 8, tn), lambda l, j: (l, 0, j)),
        out_shape=jax.ShapeDtypeStruct((depth, 8, n), F32),
        compiler_params=_cparams(("arbitrary", "arbitrary")),
        name="modulation",
    )(cond, w_mod, b_mod.reshape(depth, 1, n))


def _rope(v, cos, sin):
    lane = lax.broadcasted_iota(jnp.int32, v.shape, 1)
    partner = jnp.where((lane % 64) < 32, pltpu.roll(v, 96, 1), pltpu.roll(v, 32, 1))
    return v * cos + partner * sin


def _inproj_kernel(*refs, rope, cond_row):
    if rope:
        x_ref, v_ref, sc_ref, sh_ref, w_ref, cos_ref, sin_ref = refs[:7]
    else:
        x_ref, v_ref, sc_ref, sh_ref, w_ref = refs[:5]
    xbc_ref, z_ref, q_ref, k_ref, vt_ref, dt_ref, h_scr = refs[-7:]
    h = (_rms(x_ref[0], _vec(v_ref, V_G_PRE_MIX)) * (1.0 + _cond(sc_ref, cond_row))
         + _cond(sh_ref, cond_row))
    h_scr[...] = h.astype(BF16)

    nt = (((1,), (1,)), ((), ()))

    def proj(r0, n):
        return lax.dot_general(h_scr[...], w_ref[0, r0:r0 + n, :], nt, preferred_element_type=F32)

    seg = 512
    for s in range(D_XBC // seg):
        xbc_ref[0, :, s * seg:(s + 1) * seg] = proj(s * seg, seg).astype(BF16)
    for s in range(D_SSM // seg):
        z_ref[0, :, s * seg:(s + 1) * seg] = proj(D_XBC + s * seg, seg).astype(BF16)

    dt0 = D_XBC + D_SSM
    lane = lax.broadcasted_iota(jnp.int32, (x_ref.shape[1], LANES), 1)
    dt_ref[0] = jnp.where(lane < N_DT, proj(dt0, LANES), 0.0)

    scale = ATTN_HEAD_DIM ** -0.5 * LOG2_E
    hd = ATTN_HEAD_DIM
    q0 = dt0 + N_DT
    for s in range(D_ATTN // seg):
        acc = proj(q0 + s * seg, seg)
        for hh in range(seg // hd):
            v = acc[:, hh * hd:(hh + 1) * hd]
            if rope:
                v = _rope(v, cos_ref[...], sin_ref[...])
            q_ref[0, :, s * seg + hh * hd:s * seg + (hh + 1) * hd] = (v * scale).astype(BF16)
    acc = proj(q0 + D_ATTN, KV_DIM)
    for hh in range(ATTN_KV_HEADS):
        v = acc[:, hh * hd:(hh + 1) * hd]
        if rope:
            v = _rope(v, cos_ref[...], sin_ref[...])
        k_ref[0, :, hh * hd:(hh + 1) * hd] = v.astype(BF16)
    v0 = q0 + D_ATTN + KV_DIM
    vt_ref[0] = lax.dot_general(w_ref[0, v0:v0 + KV_DIM, :], h_scr[...], nt,
                                preferred_element_type=F32).astype(BF16)


def _inproj(x, vecs, mods, w_in_t, tables, *, layer, cond_row, tm):
    b, l, d = x.shape
    rope = tables is not None
    row = lambda w: pl.BlockSpec((1, tm, w), lambda bi, i: (bi, i, 0))
    tab = pl.BlockSpec((tm, LANES), lambda bi, i: (i, 0))
    widths = [(D_XBC, BF16), (D_SSM, BF16), (D_ATTN, BF16), (KV_DIM, BF16)]
    out_specs = [row(w) for w, _ in widths] + [pl.BlockSpec((1, KV_DIM, tm), lambda bi, i: (bi, 0, i)), row(LANES)]
    out_shape = ([jax.ShapeDtypeStruct((b, l, w), dt) for w, dt in widths]
                 + [jax.ShapeDtypeStruct((b, KV_DIM, l), BF16), jax.ShapeDtypeStruct((b, l, LANES), F32)])
    in_specs = [row(d), _vec_spec(vecs, layer), _mod_spec(mods, layer, M_SC_MIX), _mod_spec(mods, layer, M_SH_MIX),
                _const_spec((1,) + w_in_t.shape[1:], (layer, 0, 0))]
    args = [x, vecs, mods, mods, w_in_t]
    if rope:
        in_specs += [tab, tab]
        args += list(tables)
    return pl.pallas_call(
        functools.partial(_inproj_kernel, rope=rope, cond_row=cond_row),
        grid=(b, l // tm),
        in_specs=in_specs,
        out_specs=out_specs,
        out_shape=out_shape,
        scratch_shapes=[pltpu.VMEM((tm, d), BF16)],
        compiler_params=_cparams(("arbitrary", "arbitrary")),
        name="inproj_rope" if rope else "inproj",
    )(*args)


def _conv_kernel(u_ref, prev_ref, next_ref, w_ref, v_ref, o_ref, ext_scr, *, t):
    i = pl.program_id(1)
    n = pl.num_programs(1)
    halo = SSM_CONV // 2
    h16 = BF16_SUBLANES
    zero = jnp.zeros(prev_ref.shape[1:], BF16)
    ext_scr[0:h16, :] = jnp.where(i > 0, prev_ref[0], zero)
    ext_scr[h16:h16 + t, :] = u_ref[0]
    ext_scr[h16 + t:h16 + t + h16, :] = jnp.where(i < n - 1, next_ref[0], zero)
    blk = CONV_ROWS
    win = blk + 2 * h16
    rr = lax.broadcasted_iota(jnp.int32, (blk, win), 0)
    cc = lax.broadcasted_iota(jnp.int32, (blk, win), 1)
    shifts = [(cc == rr + h16 + k - halo).astype(BF16) for k in range(SSM_CONV)]
    bias = _vec(v_ref, V_CONV_B, D_XBC)
    for r in range(t // blk):
        window = ext_scr[r * blk:r * blk + win, :]
        acc = bias
        for k in range(SSM_CONV):
            acc = acc + w_ref[0, k:k + 1, :] * jnp.dot(shifts[k], window, preferred_element_type=F32)
        o_ref[0, r * blk:(r + 1) * blk, :] = _silu(acc).astype(BF16)


def _conv_silu(u, conv_w, vecs, *, layer, t):
    b, l, c = u.shape
    h16 = BF16_SUBLANES
    nh = l // h16
    per = t // h16
    return pl.pallas_call(
        functools.partial(_conv_kernel, t=t),
        grid=(b, l // t),
        in_specs=[pl.BlockSpec((1, t, c), lambda bi, i: (bi, i, 0)),
                  pl.BlockSpec((1, h16, c), lambda bi, i: (bi, jnp.maximum(i * per - 1, 0), 0)),
                  pl.BlockSpec((1, h16, c), lambda bi, i: (bi, jnp.minimum((i + 1) * per, nh - 1), 0)),
                  _const_spec((1,) + conv_w.shape[1:], (layer, 0, 0)), _vec_spec(vecs, layer)],
        out_specs=pl.BlockSpec((1, t, c), lambda bi, i: (bi, i, 0)),
        out_shape=jax.ShapeDtypeStruct((b, l, c), BF16),
        scratch_shapes=[pltpu.VMEM((t + 2 * h16, c), BF16)],
        compiler_params=_cparams(("arbitrary", "arbitrary")),
        name="conv_silu",
    )(u, u, u, conv_w, vecs)


def _split3(v):
    hi = v.astype(BF16)
    r1 = v - hi.astype(F32)
    mid = r1.astype(BF16)
    lo = (r1 - mid.astype(F32)).astype(BF16)
    return hi, mid, lo


def _ssd_kernel(xf_ref, xb_ref, dtf_ref, dtb_ref, v_ref, h0f_ref, h0b_ref,
                yf_ref, yb_ref, hff_ref, hfb_ref, ef_scr, eb_scr):
    @pl.when(pl.program_id(1) == 0)
    def _():
        r = lax.broadcasted_iota(jnp.int32, ef_scr.shape, 0) % LANES
        col = lax.broadcasted_iota(jnp.int32, ef_scr.shape, 1)
        for h0_ref, hf_ref, e_scr, off in ((h0f_ref, hff_ref, ef_scr, 0), (h0b_ref, hfb_ref, eb_scr, SSM_HEADS)):
            hf_ref[0] = h0_ref[0]
            e_scr[...] = (r == col // SSM_HEAD_DIM + off).astype(BF16)

    bias = _vec(v_ref, V_DT_BIAS, LANES)
    a_log = _vec(v_ref, V_A_LOG, LANES)
    prep_f = _ssd_prep(dtf_ref, bias, a_log, ef_scr, rev=False)
    prep_b = _ssd_prep(dtb_ref, bias, a_log, eb_scr, rev=True)
    _ssd_chunk(xf_ref, yf_ref, hff_ref, prep_f, rev=False)
    _ssd_chunk(xb_ref, yb_ref, hfb_ref, prep_b, rev=True)


def _ssd_prep(dt_ref, bias, a_log, e_scr, *, rev):
    q = SSM_CHUNK
    x = dt_ref[0] + bias
    dt = jnp.maximum(x, 0.0) + jnp.log1p(jnp.exp(-jnp.abs(x)))
    a2 = -jnp.exp(a_log) * LOG2_E
    dta = dt * a2

    ri = lax.broadcasted_iota(jnp.int32, (q, q), 0)
    ci = lax.broadcasted_iota(jnp.int32, (q, q), 1)
    mask = (ci >= ri) if rev else (ci <= ri)
    tri = mask.astype(BF16)
    cs = sum(jnp.dot(tri, part, preferred_element_type=F32) for part in _split3(dta))
    tot = cs[0:1] if rev else cs[q - 1:q]
    row_t = (cs - jnp.log2(dt)).T

    ecs = jnp.exp2(cs)
    ecs_hi = ecs.astype(BF16)
    ecs_lo = (ecs - ecs_hi.astype(F32)).astype(BF16)
    w_end = (dt * jnp.exp2(tot - cs)).astype(BF16)
    wx = jnp.dot(w_end, e_scr[0:LANES, :], preferred_element_type=F32)
    ecsx = jnp.dot(jnp.concatenate([ecs_hi, ecs_lo], axis=1), e_scr[...], preferred_element_type=F32)
    return dict(mask=mask, cs=cs, row_t=row_t, wx=wx, ecsx=ecsx)


def _ssd_chunk(xbc_ref, y_ref, hf_ref, prep, *, rev):
    q = SSM_CHUNK
    off = SSM_HEADS if rev else 0
    mask, cs, row_t, wx, ecsx = prep["mask"], prep["cs"], prep["row_t"], prep["wx"], prep["ecsx"]
    cdec = ecsx[0:1] if rev else ecsx[q - 1:q]

    lane_lo = lax.broadcasted_iota(jnp.int32, (q, LANES), 1) < SSM_HEAD_DIM
    nt = (((1,), (1,)), ((), ()))
    for g in range(SSM_GROUPS):
        gc = slice(g * GROUP_COLS, (g + 1) * GROUP_COLS)
        bg = xbc_ref[0, :, D_SSM + g * SSM_STATE:D_SSM + (g + 1) * SSM_STATE]
        cg = xbc_ref[0, :, D_SSM + SSM_BC + g * SSM_STATE:D_SSM + SSM_BC + (g + 1) * SSM_STATE]
        cb = lax.dot_general(cg, bg, nt, preferred_element_type=F32)
        s_in = hf_ref[0, :, gc]
        y_off = jnp.dot(cg, s_in.astype(BF16), preferred_element_type=F32) * ecsx[:, gc]
        xg = xbc_ref[0, :, gc]
        xw = (xg.astype(F32) * wx[:, gc]).astype(BF16)
        bg_t = bg.astype(F32).T.astype(BF16)
        hf_ref[0, :, gc] = s_in * cdec[:, gc] + jnp.dot(bg_t, xw, preferred_element_type=F32)

        for jp in range(HEADS_PER_GROUP // 2):
            pair = g * (HEADS_PER_GROUP // 2) + jp

            def head_mat(h):
                seg = cs[:, off + h:off + h + 1] - row_t[off + h:off + h + 1, :]
                return (cb * jnp.exp2(jnp.where(mask, seg, -jnp.inf))).astype(BF16)

            lhs = jnp.concatenate([head_mat(2 * pair), head_mat(2 * pair + 1)], axis=1)
            xp = xbc_ref[0, :, pair * LANES:(pair + 1) * LANES]
            zero = jnp.zeros_like(xp)
            rhs = jnp.concatenate([jnp.where(lane_lo, xp, zero), jnp.where(lane_lo, zero, xp)], axis=0)
            y_d = jnp.dot(lhs, rhs, preferred_element_type=F32)
            y_ref[0, :, pair * LANES:(pair + 1) * LANES] = (
                y_d + y_off[:, jp * LANES:(jp + 1) * LANES]).astype(BF16)


def _ssd(xbc, dt_raw, vecs, h0_f, h0_b, *, layer):
    b, l, _ = xbc.shape
    q = SSM_CHUNK
    nc = l // q
    fwd = lambda w: pl.BlockSpec((1, q, w), lambda bi, c: (bi, c, 0))
    bwd = lambda w: pl.BlockSpec((1, q, w), lambda bi, c: (bi, nc - 1 - c, 0))
    st = pl.BlockSpec((1, SSM_STATE, D_SSM), lambda bi, c: (bi, 0, 0))
    y_shape = jax.ShapeDtypeStruct((b, l, D_SSM), BF16)
    h_shape = jax.ShapeDtypeStruct((b, SSM_STATE, D_SSM), F32)
    return pl.pallas_call(
        _ssd_kernel,
        grid=(b, nc),
        in_specs=[fwd(D_XBC), bwd(D_XBC), fwd(LANES), bwd(LANES), _vec_spec(vecs, layer), st, st],
        out_specs=[fwd(D_SSM), bwd(D_SSM), st, st],
        out_shape=[y_shape, y_shape, h_shape, h_shape],
        scratch_shapes=[pltpu.VMEM((2 * LANES, D_SSM), BF16), pltpu.VMEM((2 * LANES, D_SSM), BF16)],
        compiler_params=_cparams(("arbitrary", "arbitrary")),
        name="ssd",
    )(xbc, xbc, dt_raw, dt_raw, vecs, h0_f, h0_b)


def _attn_kernel(*refs, band):
    if band:
        q_ref, kp_ref, kc_ref, kn_ref, vp_ref, vc_ref, vn_ref, kx_ref, vx_ref, v_ref, o_ref = refs
    else:
        q_ref, kx_ref, vx_ref, v_ref, o_ref = refs
    i = pl.program_id(1)
    t = ATTN_BLOCK
    hd = ATTN_HEAD_DIM
    rep = ATTN_HEADS // ATTN_KV_HEADS
    tq = q_ref.shape[1]
    nsub = tq // t if band else 1
    rows_per = t if band else tq
    nblk = pl.num_programs(1) * nsub
    lc = kx_ref.shape[1]
    nk = (3 * t if band else 0) + lc
    nq = rep * rows_per
    nt = (((1,), (1,)), ((), ()))

    if band:
        krow = lax.broadcasted_iota(jnp.int32, (t, nq), 0)
        qcol = lax.broadcasted_iota(jnp.int32, (t, nq), 1) % t
    qhead = lax.broadcasted_iota(jnp.int32, (1, nq), 1) // rows_per
    ones = jnp.ones((BF16_SUBLANES, nk), BF16)
    sinks = _vec(v_ref, V_SINK, LANES) * LOG2_E

    def scores(g, j):
        ks = slice(g * hd, (g + 1) * hd)
        rows = slice(j * rows_per, (j + 1) * rows_per)
        qg = jnp.concatenate([q_ref[0, rows, (g * rep + r) * hd:(g * rep + r + 1) * hd] for r in range(rep)],
                             axis=0)
        if band:
            band_k = jnp.concatenate([kp_ref[0, :, ks], kc_ref[0, :, ks], kn_ref[0, :, ks]], axis=0)
            k_all = jnp.concatenate([band_k[j * t:(j + 3) * t], kx_ref[0, :, ks]], axis=0)
        else:
            k_all = kx_ref[0, :, ks]
        s = lax.dot_general(k_all, qg, nt, preferred_element_type=F32)
        if band:
            qb = i * nsub + j
            lo = qcol + jnp.where(qb > 0, 0, 2 * t)
            hi = qcol - jnp.where(qb < nblk - 1, 0, 2 * t)
            s = jnp.concatenate([jnp.where(krow >= lo, s[0:t], -jnp.inf), s[t:2 * t],
                                 jnp.where(krow <= hi, s[2 * t:3 * t], -jnp.inf), s[3 * t:]], axis=0)
        return s

    def finish(g, j, s):
        ks = slice(g * hd, (g + 1) * hd)
        rows = slice(j * rows_per, (j + 1) * rows_per)
        sink = jnp.zeros((1, nq), F32)
        for r in range(rep):
            sink = jnp.where(qhead == r, sinks[:, g * rep + r:g * rep + r + 1], sink)
        if band:
            band_v = jnp.concatenate([vp_ref[0, ks, :], vc_ref[0, ks, :], vn_ref[0, ks, :]], axis=1)
            v_all = jnp.concatenate([band_v[:, j * t:(j + 3) * t], vx_ref[0, ks, :]], axis=1)
        else:
            v_all = vx_ref[0, ks, :]
        m = jnp.maximum(jnp.max(s, axis=0, keepdims=True), sink)
        p = jnp.exp2(s - m).astype(BF16)
        ov = jnp.dot(jnp.concatenate([v_all, ones], axis=0), p, preferred_element_type=F32)
        inv = 1.0 / (ov[hd:hd + 1] + jnp.exp2(sink - m))
        o = ov[0:hd] * inv
        for r in range(rep):
            o_ref[0, rows, (g * rep + r) * hd:(g * rep + r + 1) * hd] = (
                o[:, r * rows_per:(r + 1) * rows_per].T.astype(BF16))

    chains = [(g, j) for g in range(ATTN_KV_HEADS) for j in range(nsub)]
    s_next = scores(*chains[0])
    for n, (g, j) in enumerate(chains):
        s_cur = s_next
        if n + 1 < len(chains):
            s_next = scores(*chains[n + 1])
        finish(g, j, s_cur)


def _attention(q, k, vt, kx, vtx, vecs, *, layer, band):
    b, l, _ = q.shape
    t = ATTN_BLOCK
    tq = min(l, 2 * t)
    nsub = tq // t
    nblk = l // t
    lc = kx.shape[1]
    qs = pl.BlockSpec((1, tq, D_ATTN), lambda bi, i: (bi, i, 0))
    kctx = pl.BlockSpec((1, lc, KV_DIM), lambda bi, i: (bi, 0, 0))
    vctx = pl.BlockSpec((1, KV_DIM, lc), lambda bi, i: (bi, 0, 0))
    vs = _vec_spec(vecs, layer)
    if band:
        prv = lambda i: jnp.maximum(i * nsub - 1, 0)
        nxt = lambda i: jnp.minimum((i + 1) * nsub, nblk - 1)
        kspec = lambda rows, f: pl.BlockSpec((1, rows, KV_DIM), lambda bi, i: (bi, f(i), 0))
        vspec = lambda rows, f: pl.BlockSpec((1, KV_DIM, rows), lambda bi, i: (bi, 0, f(i)))
        cur = lambda i: i
        in_specs = [qs, kspec(t, prv), kspec(tq, cur), kspec(t, nxt),
                    vspec(t, prv), vspec(tq, cur), vspec(t, nxt), kctx, vctx, vs]
        args = (q, k, k, k, vt, vt, vt, kx, vtx, vecs)
    else:
        in_specs = [qs, kctx, vctx, vs]
        args = (q, kx, vtx, vecs)
    return pl.pallas_call(
        functools.partial(_attn_kernel, band=band),
        grid=(b, l // tq),
        in_specs=in_specs,
        out_specs=qs,
        out_shape=jax.ShapeDtypeStruct((b, l, D_ATTN), BF16),
        compiler_params=_cparams(("arbitrary", "arbitrary")),
        name="attn_band" if band else "attn_ctx",
    )(*args)


def _outproj_kernel(yf_ref, yb_ref, xs_ref, z_ref, ya_ref, x_ref, w_ref, v_ref, gate_ref, scf_ref, shf_ref,
                    xo_ref, hm_ref, *, sub, cond_row):
    d_skip = _vec(v_ref, V_DSKIP, D_SSM)
    g_ssm = _vec(v_ref, V_SSM_NORM, D_SSM)
    gate = _cond(gate_ref, cond_row) * _vec(v_ref, V_G_POST_MIX)
    g_pre = _vec(v_ref, V_G_PRE_MLP) * (1.0 + _cond(scf_ref, cond_row))
    shift = _cond(shf_ref, cond_row)
    for r in range(x_ref.shape[1] // sub):
        rows = slice(r * sub, (r + 1) * sub)
        y = (yf_ref[0, rows, :].astype(F32) + yb_ref[0, rows, :].astype(F32)
             + xs_ref[0, rows, :].astype(F32) * d_skip)
        y = y * _silu(z_ref[0, rows, :].astype(F32))
        cat = jnp.concatenate([_rms(y, g_ssm).astype(BF16), ya_ref[0, rows, :]], axis=1)
        mix = jnp.dot(cat, w_ref[0], preferred_element_type=F32)
        xn = x_ref[0, rows, :] + _unit(mix) * gate
        xo_ref[0, rows, :] = xn
        hm_ref[0, rows, :] = (_unit(xn) * g_pre + shift).astype(BF16)


def _outproj(yf, yb, xbc, z, ya, x, w_out_b, vecs, mods, *, layer, cond_row, tm):
    b, l, d = x.shape
    row = lambda w: pl.BlockSpec((1, tm, w), lambda bi, i: (bi, i, 0))
    return pl.pallas_call(
        functools.partial(_outproj_kernel, sub=min(tm, 256), cond_row=cond_row),
        grid=(b, l // tm),
        in_specs=[row(D_SSM), row(D_SSM), row(D_SSM), row(D_SSM), row(D_ATTN), row(d),
                  _const_spec((1,) + w_out_b.shape[1:], (layer, 0, 0)), _vec_spec(vecs, layer),
                  _mod_spec(mods, layer, M_G_MIX), _mod_spec(mods, layer, M_SC_MLP),
                  _mod_spec(mods, layer, M_SH_MLP)],
        out_specs=[row(d), row(d)],
        out_shape=[jax.ShapeDtypeStruct((b, l, d), F32), jax.ShapeDtypeStruct((b, l, d), BF16)],
        compiler_params=_cparams(("arbitrary", "arbitrary")),
        name="outproj",
    )(yf, yb, xbc, z, ya, x, w_out_b, vecs, mods, mods, mods)


def _mlp_kernel(h_ref, x_ref, w1_ref, w2_ref, v_ref, gate_ref, xo_ref, acc_ref, *, cond_row):
    k = pl.program_id(2)

    @pl.when(k == 0)
    def _():
        acc_ref[...] = jnp.zeros_like(acc_ref)

    a = jnp.dot(h_ref[0], w1_ref[0], preferred_element_type=F32)
    a = jnp.square(jnp.maximum(a, 0.0)).astype(BF16)
    acc_ref[...] += jnp.dot(a, w2_ref[0], preferred_element_type=F32)

    @pl.when(k == pl.num_programs(2) - 1)
    def _():
        xo_ref[0] = x_ref[0] + _cond(gate_ref, cond_row) * _rms(acc_ref[...], _vec(v_ref, V_G_POST_MLP))


def _mlp(h, x, w1_b, w2_b, vecs, mods, *, layer, cond_row, tm, tf):
    b, l, d = x.shape
    dff = w1_b.shape[2]
    row = pl.BlockSpec((1, tm, d), lambda bi, i, k: (bi, i, 0))
    return pl.pallas_call(
        functools.partial(_mlp_kernel, cond_row=cond_row),
        grid=(b, l // tm, dff // tf),
        in_specs=[row, row,
                  pl.BlockSpec((1, d, tf), lambda bi, i, k: (layer, 0, k)),
                  pl.BlockSpec((1, tf, d), lambda bi, i, k: (layer, k, 0)),
                  _vec_spec(vecs, layer), _mod_spec(mods, layer, M_G_MLP)],
        out_specs=row,
        out_shape=jax.ShapeDtypeStruct((b, l, d), F32),
        scratch_shapes=[pltpu.VMEM((tm, d), F32)],
        compiler_params=_cparams(("arbitrary", "arbitrary", "arbitrary")),
        name="mlp",
    )(h, x, w1_b, w2_b, vecs, mods)


def _rope_tables(l):
    n_freq = ATTN_HEAD_DIM // 4
    inv_freq = ROPE_BASE ** (-np.arange(n_freq, dtype=np.float64) / n_freq)
    t = np.arange(l)
    ang_r = (t // GRID_W)[:, None] * inv_freq[None, :]
    ang_c = (t % GRID_W)[:, None] * inv_freq[None, :]
    cos = np.concatenate([np.cos(ang_r)] * 2 + [np.cos(ang_c)] * 2, axis=-1)
    sin = np.concatenate([-np.sin(ang_r), np.sin(ang_r), -np.sin(ang_c), np.sin(ang_c)], axis=-1)
    return jnp.asarray(cos, F32), jnp.asarray(sin, F32)


def _pack_vectors(d, g_pre_mix, g_post_mix, g_pre_mlp, g_post_mlp, ssm_norm, d_skip, conv_b, dt_bias, a_log,
                  attn_sink):
    depth = g_pre_mix.shape[0]

    def r(v):
        v = v.reshape(depth, 1, -1).astype(F32)
        return jnp.pad(v, ((0, 0), (0, 0), (0, d - v.shape[-1])))

    rows = [None] * VEC_ROWS
    rows[V_G_PRE_MIX], rows[V_G_POST_MIX] = r(g_pre_mix), r(g_post_mix)
    rows[V_G_PRE_MLP], rows[V_G_POST_MLP] = r(g_pre_mlp), r(g_post_mlp)
    rows[V_SSM_NORM], rows[V_DSKIP] = r(ssm_norm), r(jnp.repeat(d_skip, SSM_HEAD_DIM, axis=-1))
    rows[V_CONV_B], rows[V_DT_BIAS], rows[V_A_LOG], rows[V_SINK] = r(conv_b), r(dt_bias), r(a_log), r(attn_sink)
    zero = jnp.zeros((depth, 1, d), F32)
    return jnp.concatenate([zero if v is None else v for v in rows], axis=1)


def kernel(x, c, ctx, c_ctx, w_mod, b_mod, g_pre_mix, g_post_mix, g_pre_mlp, g_post_mlp, w_in, conv_w,
           conv_b, a_log, dt_bias, d_skip, ssm_norm, attn_sink, w_out, w_ff1, w_ff2):
    b, l, d = x.shape
    lc = ctx.shape[1]
    depth = w_mod.shape[0]
    assert b <= 7 and l % ROW_TILE == 0 and lc % SSM_CHUNK == 0 and w_in.shape[2] == D_IN_PROJ

    cond = jnp.concatenate([c, c_ctx[None, :], jnp.zeros((8 - b - 1, d), F32)], axis=0)
    ctx_row = b
    mods = _modulation(cond, w_mod, b_mod)
    vecs = _pack_vectors(d, g_pre_mix, g_post_mix, g_pre_mlp, g_post_mlp, ssm_norm, d_skip, conv_b, dt_bias,
                         a_log, attn_sink)
    w_in_t = jnp.swapaxes(w_in, 1, 2).astype(BF16)
    w_out_b = w_out.astype(BF16)
    w1_b = w_ff1.astype(BF16)
    w2_b = w_ff2.astype(BF16)
    tables = _rope_tables(l)
    h_zero = jnp.zeros((b, SSM_STATE, D_SSM), F32)

    tm_c = min(lc, ROW_TILE)
    xc = ctx
    for i in range(depth):
        last = i == depth - 1
        xbc_c, z_c, q_c, k_c, vt_c, dt_c = _inproj(xc, vecs, mods, w_in_t, None, layer=i, cond_row=ctx_row, tm=tm_c)
        xbc_c = _conv_silu(xbc_c, conv_w, vecs, layer=i, t=tm_c)
        yf_c, yb_c, h_f, h_b = _ssd(xbc_c, dt_c, vecs, h_zero, h_zero, layer=i)

        xbc_l, z_l, q_l, k_l, vt_l, dt_l = _inproj(x, vecs, mods, w_in_t, tables, layer=i, cond_row=None,
                                                   tm=ROW_TILE)
        xbc_l = _conv_silu(xbc_l, conv_w, vecs, layer=i, t=ROW_TILE)
        yf, yb, _, _ = _ssd(xbc_l, dt_l, vecs, h_f, h_b, layer=i)
        ya = _attention(q_l, k_l, vt_l, k_c, vt_c, vecs, layer=i, band=True)
        x, hm = _outproj(yf, yb, xbc_l, z_l, ya, x, w_out_b, vecs, mods, layer=i, cond_row=None, tm=ROW_TILE)
        x = _mlp(hm, x, w1_b, w2_b, vecs, mods, layer=i, cond_row=None, tm=ROW_TILE, tf=FF_TILE)

        if not last:
            ya_c = _attention(q_c, None, None, k_c, vt_c, vecs, layer=i, band=False)
            xc, hm_c = _outproj(yf_c, yb_c, xbc_c, z_c, ya_c, xc, w_out_b, vecs, mods, layer=i, cond_row=ctx_row,
                                tm=tm_c)
            xc = _mlp(hm_c.reshape(1, b * lc, d), xc.reshape(1, b * lc, d), w1_b, w2_b, vecs, mods, layer=i,
                      cond_row=ctx_row, tm=min(b * lc, ROW_TILE), tf=FF_TILE).reshape(b, lc, d)
    return x
```

```python
import functools

import numpy as np
import jax
import jax.numpy as jnp
from jax import lax
from jax.experimental import pallas as pl
from jax.experimental.pallas import tpu as pltpu

F32 = jnp.float32
BF16 = jnp.bfloat16

N_MOD = 6
SSM_HEAD_DIM = 64
SSM_HEADS = 16
SSM_STATE = 128
SSM_GROUPS = 2
SSM_CONV = 5
SSM_CHUNK = 128
ATTN_HEAD_DIM = 128
ATTN_HEADS = 8
ATTN_KV_HEADS = 2
ATTN_BLOCK = 128
GRID_W = 64
ROPE_BASE = 10000.0
EPS = 1e-6
LOG2_E = 1.4426950408889634

D_SSM = SSM_HEADS * SSM_HEAD_DIM
SSM_BC = SSM_GROUPS * SSM_STATE
D_XBC = D_SSM + 2 * SSM_BC
D_ATTN = ATTN_HEADS * ATTN_HEAD_DIM
KV_DIM = ATTN_KV_HEADS * ATTN_HEAD_DIM
N_DT = 2 * SSM_HEADS
D_IN_PROJ = D_XBC + D_SSM + N_DT + D_ATTN + 2 * KV_DIM
HEADS_PER_GROUP = SSM_HEADS // SSM_GROUPS
GROUP_COLS = HEADS_PER_GROUP * SSM_HEAD_DIM

LANES = 128
BF16_SUBLANES = 16
VMEM_LIMIT_BYTES = 56 * 1024 * 1024

ROW_TILE = 512
FF_TILE = 1024
CONV_ROWS = 128
SSD_CHUNKS_PER_STEP = 4

V_G_PRE_MIX, V_G_POST_MIX, V_G_PRE_MLP, V_G_POST_MLP = 0, 1, 2, 3
V_SSM_NORM, V_DSKIP, V_CONV_B, V_DT_BIAS, V_A_LOG, V_SINK = 4, 5, 6, 7, 8, 9
VEC_ROWS = 16
M_SH_MIX, M_SC_MIX, M_G_MIX, M_SH_MLP, M_SC_MLP, M_G_MLP = range(N_MOD)


def _cparams(sem):
    return pltpu.CompilerParams(dimension_semantics=sem, vmem_limit_bytes=VMEM_LIMIT_BYTES)


def _unit(xf):
    ms = jnp.mean(xf * xf, axis=-1, keepdims=True)
    return xf * lax.rsqrt(ms + EPS)


def _rms(xf, g):
    return _unit(xf) * g


def _silu(v):
    return v * jax.nn.sigmoid(v)


def _const_spec(block, index):
    return pl.BlockSpec(block, lambda *_: index, pipeline_mode=pl.Buffered(1))


def _vec_spec(vecs, layer):
    return _const_spec((1,) + vecs.shape[1:], (layer, 0, 0))


def _mod_spec(mods, layer, chunk):
    d = mods.shape[-1] // N_MOD
    return _const_spec((1, mods.shape[1], d), (layer, 0, chunk))


def _vec(v_ref, row, width=None):
    return v_ref[0, row:row + 1, :] if width is None else v_ref[0, row:row + 1, 0:width]


def _cond(m_ref, cond_row):
    if cond_row is None:
        return m_ref[0, pl.ds(pl.program_id(0), 1), :]
    return m_ref[0, cond_row:cond_row + 1, :]


def _mod_kernel(c_ref, w_ref, b_ref, o_ref):
    s = _silu(c_ref[...]).astype(BF16)
    o_ref[0] = jnp.dot(s, w_ref[0].astype(BF16), preferred_element_type=F32) + b_ref[0]


def _modulation(cond, w_mod, b_mod):
    depth, d, n = w_mod.shape
    tn = 1024
    return pl.pallas_call(
        _mod_kernel,
        grid=(depth, n // tn),
        in_specs=[pl.BlockSpec((8, d), lambda l, j: (0, 0)),
                  pl.BlockSpec((1, d, tn), lambda l, j: (l, 0, j)),
                  pl.BlockSpec((1, 1, tn), lambda l, j: (l, 0, j))],
        out_specs=pl.BlockSpec((1, 8, tn), lambda l, j: (l, 0, j)),
        out_shape=jax.ShapeDtypeStruct((depth, 8, n), F32),
        compiler_params=_cparams(("arbitrary", "arbitrary")),
        name="modulation",
    )(cond, w_mod, b_mod.reshape(depth, 1, n))


def _rope(v, cos, sin):
    lane = lax.broadcasted_iota(jnp.int32, v.shape, 1)
    partner = jnp.where((lane % 64) < 32, pltpu.roll(v, 96, 1), pltpu.roll(v, 32, 1))
    return v * cos + partner * sin


def _inproj_kernel(*refs, rope, cond_row):
    if rope:
        x_ref, v_ref, sc_ref, sh_ref, w_ref, cos_ref, sin_ref = refs[:7]
    else:
        x_ref, v_ref, sc_ref, sh_ref, w_ref = refs[:5]
    xbc_ref, z_ref, q_ref, k_ref, vt_ref, dt_ref, h_scr = refs[-7:]
    h = (_rms(x_ref[0], _vec(v_ref, V_G_PRE_MIX)) * (1.0 + _cond(sc_ref, cond_row))
         + _cond(sh_ref, cond_row))
    h_scr[...] = h.astype(BF16)

    nt = (((1,), (1,)), ((), ()))

    def proj(r0, n):
        return lax.dot_general(h_scr[...], w_ref[0, r0:r0 + n, :], nt, preferred_element_type=F32)

    seg = 512
    for s in range(D_XBC // seg):
        xbc_ref[0, :, s * seg:(s + 1) * seg] = proj(s * seg, seg).astype(BF16)
    for s in range(D_SSM // seg):
        z_ref[0, :, s * seg:(s + 1) * seg] = proj(D_XBC + s * seg, seg).astype(BF16)

    dt0 = D_XBC + D_SSM
    lane = lax.broadcasted_iota(jnp.int32, (x_ref.shape[1], LANES), 1)
    dt_ref[0] = jnp.where(lane < N_DT, proj(dt0, LANES), 0.0)

    scale = ATTN_HEAD_DIM ** -0.5 * LOG2_E
    hd = ATTN_HEAD_DIM
    q0 = dt0 + N_DT
    for s in range(D_ATTN // seg):
        acc = proj(q0 + s * seg, seg)
        for hh in range(seg // hd):
            v = acc[:, hh * hd:(hh + 1) * hd]
            if rope:
                v = _rope(v, cos_ref[...], sin_ref[...])
            q_ref[0, :, s * seg + hh * hd:s * seg + (hh + 1) * hd] = (v * scale).astype(BF16)
    acc = proj(q0 + D_ATTN, KV_DIM)
    for hh in range(ATTN_KV_HEADS):
        v = acc[:, hh * hd:(hh + 1) * hd]
        if rope:
            v = _rope(v, cos_ref[...], sin_ref[...])
        k_ref[0, :, hh * hd:(hh + 1) * hd] = v.astype(BF16)
    v0 = q0 + D_ATTN + KV_DIM
    vt_ref[0] = lax.dot_general(w_ref[0, v0:v0 + KV_DIM, :], h_scr[...], nt,
                                preferred_element_type=F32).astype(BF16)


def _inproj(x, vecs, mods, w_in_t, tables, *, layer, cond_row, tm):
    b, l, d = x.shape
    rope = tables is not None
    row = lambda w: pl.BlockSpec((1, tm, w), lambda bi, i: (bi, i, 0))
    tab = pl.BlockSpec((tm, LANES), lambda bi, i: (i, 0))
    widths = [(D_XBC, BF16), (D_SSM, BF16), (D_ATTN, BF16), (KV_DIM, BF16)]
    out_specs = [row(w) for w, _ in widths] + [pl.BlockSpec((1, KV_DIM, tm), lambda bi, i: (bi, 0, i)), row(LANES)]
    out_shape = ([jax.ShapeDtypeStruct((b, l, w), dt) for w, dt in widths]
                 + [jax.ShapeDtypeStruct((b, KV_DIM, l), BF16), jax.ShapeDtypeStruct((b, l, LANES), F32)])
    in_specs = [row(d), _vec_spec(vecs, layer), _mod_spec(mods, layer, M_SC_MIX), _mod_spec(mods, layer, M_SH_MIX),
                _const_spec((1,) + w_in_t.shape[1:], (layer, 0, 0))]
    args = [x, vecs, mods, mods, w_in_t]
    if rope:
        in_specs += [tab, tab]
        args += list(tables)
    return pl.pallas_call(
        functools.partial(_inproj_kernel, rope=rope, cond_row=cond_row),
        grid=(b, l // tm),
        in_specs=in_specs,
        out_specs=out_specs,
        out_shape=out_shape,
        scratch_shapes=[pltpu.VMEM((tm, d), BF16)],
        compiler_params=_cparams(("arbitrary", "arbitrary")),
        name="inproj_rope" if rope else "inproj",
    )(*args)


def _conv_kernel(u_ref, prev_ref, next_ref, w_ref, v_ref, o_ref, ext_scr, *, t):
    i = pl.program_id(1)
    n = pl.num_programs(1)
    halo = SSM_CONV // 2
    h16 = BF16_SUBLANES
    zero = jnp.zeros(prev_ref.shape[1:], BF16)
    ext_scr[0:h16, :] = jnp.where(i > 0, prev_ref[0], zero)
    ext_scr[h16:h16 + t, :] = u_ref[0]
    ext_scr[h16 + t:h16 + t + h16, :] = jnp.where(i < n - 1, next_ref[0], zero)
    blk = CONV_ROWS
    win = blk + 2 * h16
    rr = lax.broadcasted_iota(jnp.int32, (blk, SSM_CONV * win), 0)
    cc = lax.broadcasted_iota(jnp.int32, (blk, SSM_CONV * win), 1)
    hit = cc == rr + h16 - halo
    for k in range(1, SSM_CONV):
        hit = hit | (cc == rr + k * win + h16 + k - halo)
    shift_all = hit.astype(F32).astype(BF16)
    taps = w_ref[0].astype(BF16)
    bias = _vec(v_ref, V_CONV_B, D_XBC)
    for r in range(t // blk):
        window = ext_scr[r * blk:r * blk + win, :]
        scaled = jnp.concatenate([window * taps[k:k + 1, :] for k in range(SSM_CONV)], axis=0)
        acc = jnp.dot(shift_all, scaled, preferred_element_type=F32) + bias
        o_ref[0, r * blk:(r + 1) * blk, :] = _silu(acc).astype(BF16)


def _conv_silu(u, conv_w, vecs, *, layer, t):
    b, l, c = u.shape
    h16 = BF16_SUBLANES
    nh = l // h16
    per = t // h16
    return pl.pallas_call(
        functools.partial(_conv_kernel, t=t),
        grid=(b, l // t),
        in_specs=[pl.BlockSpec((1, t, c), lambda bi, i: (bi, i, 0)),
                  pl.BlockSpec((1, h16, c), lambda bi, i: (bi, jnp.maximum(i * per - 1, 0), 0)),
                  pl.BlockSpec((1, h16, c), lambda bi, i: (bi, jnp.minimum((i + 1) * per, nh - 1), 0)),
                  _const_spec((1,) + conv_w.shape[1:], (layer, 0, 0)), _vec_spec(vecs, layer)],
        out_specs=pl.BlockSpec((1, t, c), lambda bi, i: (bi, i, 0)),
        out_shape=jax.ShapeDtypeStruct((b, l, c), BF16),
        scratch_shapes=[pltpu.VMEM((t + 2 * h16, c), BF16)],
        compiler_params=_cparams(("arbitrary", "arbitrary")),
        name="conv_silu",
    )(u, u, u, conv_w, vecs)


def _split3(v):
    hi = v.astype(BF16)
    r1 = v - hi.astype(F32)
    mid = r1.astype(BF16)
    lo = (r1 - mid.astype(F32)).astype(BF16)
    return hi, mid, lo


def _ssd_kernel(xf_ref, xb_ref, dtf_ref, dtb_ref, v_ref, h0f_ref, h0b_ref,
                yf_ref, yb_ref, hff_ref, hfb_ref, ef_scr, eb_scr):
    @pl.when(pl.program_id(1) == 0)
    def _():
        r = lax.broadcasted_iota(jnp.int32, ef_scr.shape, 0) % LANES
        col = lax.broadcasted_iota(jnp.int32, ef_scr.shape, 1)
        for h0_ref, hf_ref, e_scr, off in ((h0f_ref, hff_ref, ef_scr, 0), (h0b_ref, hfb_ref, eb_scr, SSM_HEADS)):
            hf_ref[0] = h0_ref[0]
            e_scr[...] = (r == col // SSM_HEAD_DIM + off).astype(BF16)

    bias = _vec(v_ref, V_DT_BIAS, LANES)
    a_log = _vec(v_ref, V_A_LOG, LANES)
    q = SSM_CHUNK
    n = xf_ref.shape[1] // q
    order = [(j, n - 1 - j) for j in range(n)]
    preps = [(_ssd_prep(dtf_ref[0, jf * q:(jf + 1) * q, :], bias, a_log, ef_scr, rev=False),
              _ssd_prep(dtb_ref[0, jb * q:(jb + 1) * q, :], bias, a_log, eb_scr, rev=True)) for jf, jb in order]
    for (jf, jb), (prep_f, prep_b) in zip(order, preps):
        _ssd_chunk(xf_ref, yf_ref, hff_ref, prep_f, slice(jf * q, (jf + 1) * q), rev=False)
        _ssd_chunk(xb_ref, yb_ref, hfb_ref, prep_b, slice(jb * q, (jb + 1) * q), rev=True)


def _ssd_prep(dt_raw, bias, a_log, e_scr, *, rev):
    q = SSM_CHUNK
    x = dt_raw + bias
    dt = jnp.maximum(x, 0.0) + jnp.log1p(jnp.exp(-jnp.abs(x)))
    a2 = -jnp.exp(a_log) * LOG2_E
    dta = dt * a2

    ri = lax.broadcasted_iota(jnp.int32, (q, q), 0)
    ci = lax.broadcasted_iota(jnp.int32, (q, q), 1)
    mask = (ci >= ri) if rev else (ci <= ri)
    tri = mask.astype(BF16)
    cs = sum(jnp.dot(tri, part, preferred_element_type=F32) for part in _split3(dta))
    tot = cs[0:1] if rev else cs[q - 1:q]
    row_t = (cs - jnp.log2(dt)).T

    ecs = jnp.exp2(cs)
    ecs_hi = ecs.astype(BF16)
    ecs_lo = (ecs - ecs_hi.astype(F32)).astype(BF16)
    w_end = (dt * jnp.exp2(tot - cs)).astype(BF16)
    wx = jnp.dot(w_end, e_scr[0:LANES, :], preferred_element_type=F32)
    ecsx = jnp.dot(jnp.concatenate([ecs_hi, ecs_lo], axis=1), e_scr[...], preferred_element_type=F32)
    return dict(mask=mask, cs=cs, row_t=row_t, wx=wx, ecsx=ecsx)


def _ssd_chunk(xbc_ref, y_ref, hf_ref, prep, rows, *, rev):
    q = SSM_CHUNK
    off = SSM_HEADS if rev else 0
    mask, cs, row_t, wx, ecsx = prep["mask"], prep["cs"], prep["row_t"], prep["wx"], prep["ecsx"]
    cdec = ecsx[0:1] if rev else ecsx[q - 1:q]

    lane_lo = lax.broadcasted_iota(jnp.int32, (q, LANES), 1) < SSM_HEAD_DIM
    nt = (((1,), (1,)), ((), ()))
    for g in range(SSM_GROUPS):
        gc = slice(g * GROUP_COLS, (g + 1) * GROUP_COLS)
        bg = xbc_ref[0, rows, D_SSM + g * SSM_STATE:D_SSM + (g + 1) * SSM_STATE]
        cg = xbc_ref[0, rows, D_SSM + SSM_BC + g * SSM_STATE:D_SSM + SSM_BC + (g + 1) * SSM_STATE]
        cb = lax.dot_general(cg, bg, nt, preferred_element_type=F32)
        s_in = hf_ref[0, :, gc]
        y_off = jnp.dot(cg, s_in.astype(BF16), preferred_element_type=F32) * ecsx[:, gc]
        xg = xbc_ref[0, rows, gc]
        xw = (xg.astype(F32) * wx[:, gc]).astype(BF16)
        bg_t = bg.astype(F32).T.astype(BF16)
        hf_ref[0, :, gc] = s_in * cdec[:, gc] + jnp.dot(bg_t, xw, preferred_element_type=F32)

        for jp in range(HEADS_PER_GROUP // 2):
            pair = g * (HEADS_PER_GROUP // 2) + jp

            def head_mat(h):
                seg = cs[:, off + h:off + h + 1] - row_t[off + h:off + h + 1, :]
                return (cb * jnp.exp2(jnp.where(mask, seg, -jnp.inf))).astype(BF16)

            lhs = jnp.concatenate([head_mat(2 * pair), head_mat(2 * pair + 1)], axis=1)
            xp = xbc_ref[0, rows, pair * LANES:(pair + 1) * LANES]
            zero = jnp.zeros_like(xp)
            rhs = jnp.concatenate([jnp.where(lane_lo, xp, zero), jnp.where(lane_lo, zero, xp)], axis=0)
            y_d = jnp.dot(lhs, rhs, preferred_element_type=F32)
            y_ref[0, rows, pair * LANES:(pair + 1) * LANES] = (
                y_d + y_off[:, jp * LANES:(jp + 1) * LANES]).astype(BF16)


def _ssd(xbc, dt_raw, vecs, h0_f, h0_b, *, layer):
    b, l, _ = xbc.shape
    q = min(l, SSM_CHUNK * SSD_CHUNKS_PER_STEP)
    nc = l // q
    fwd = lambda w: pl.BlockSpec((1, q, w), lambda bi, c: (bi, c, 0))
    bwd = lambda w: pl.BlockSpec((1, q, w), lambda bi, c: (bi, nc - 1 - c, 0))
    st = pl.BlockSpec((1, SSM_STATE, D_SSM), lambda bi, c: (bi, 0, 0))
    y_shape = jax.ShapeDtypeStruct((b, l, D_SSM), BF16)
    h_shape = jax.ShapeDtypeStruct((b, SSM_STATE, D_SSM), F32)
    return pl.pallas_call(
        _ssd_kernel,
        grid=(b, nc),
        in_specs=[fwd(D_XBC), bwd(D_XBC), fwd(LANES), bwd(LANES), _vec_spec(vecs, layer), st, st],
        out_specs=[fwd(D_SSM), bwd(D_SSM), st, st],
        out_shape=[y_shape, y_shape, h_shape, h_shape],
        scratch_shapes=[pltpu.VMEM((2 * LANES, D_SSM), BF16), pltpu.VMEM((2 * LANES, D_SSM), BF16)],
        compiler_params=_cparams(("arbitrary", "arbitrary")),
        name="ssd",
    )(xbc, xbc, dt_raw, dt_raw, vecs, h0_f, h0_b)


def _attn_kernel(*refs, band):
    if band:
        q_ref, kp_ref, kc_ref, kn_ref, vp_ref, vc_ref, vn_ref, kx_ref, vx_ref, v_ref, o_ref = refs
    else:
        q_ref, kx_ref, vx_ref, v_ref, o_ref = refs
    i = pl.program_id(1)
    t = ATTN_BLOCK
    hd = ATTN_HEAD_DIM
    rep = ATTN_HEADS // ATTN_KV_HEADS
    tq = q_ref.shape[1]
    nsub = tq // t if band else 1
    rows_per = t if band else tq
    nblk = pl.num_programs(1) * nsub
    lc = kx_ref.shape[1]
    nk = (3 * t if band else 0) + lc
    nq = rep * rows_per
    nt = (((1,), (1,)), ((), ()))

    if band:
        krow = lax.broadcasted_iota(jnp.int32, (t, nq), 0)
        qcol = lax.broadcasted_iota(jnp.int32, (t, nq), 1) % t
    qhead = lax.broadcasted_iota(jnp.int32, (1, nq), 1) // rows_per
    ones = jnp.ones((BF16_SUBLANES, nk), BF16)
    sinks = _vec(v_ref, V_SINK, LANES) * LOG2_E

    def scores(g, j):
        ks = slice(g * hd, (g + 1) * hd)
        rows = slice(j * rows_per, (j + 1) * rows_per)
        qg = jnp.concatenate([q_ref[0, rows, (g * rep + r) * hd:(g * rep + r + 1) * hd] for r in range(rep)],
                             axis=0)
        if band:
            band_k = jnp.concatenate([kp_ref[0, :, ks], kc_ref[0, :, ks], kn_ref[0, :, ks]], axis=0)
            k_all = jnp.concatenate([band_k[j * t:(j + 3) * t], kx_ref[0, :, ks]], axis=0)
        else:
            k_all = kx_ref[0, :, ks]
        s = lax.dot_general(k_all, qg, nt, preferred_element_type=F32)
        if band:
            qb = i * nsub + j
            lo = qcol + jnp.where(qb > 0, 0, 2 * t)
            hi = qcol - jnp.where(qb < nblk - 1, 0, 2 * t)
            s = jnp.concatenate([jnp.where(krow >= lo, s[0:t], -jnp.inf), s[t:2 * t],
                                 jnp.where(krow <= hi, s[2 * t:3 * t], -jnp.inf), s[3 * t:]], axis=0)
        return s

    def finish(g, j, s):
        ks = slice(g * hd, (g + 1) * hd)
        rows = slice(j * rows_per, (j + 1) * rows_per)
        sink = jnp.zeros((1, nq), F32)
        for r in range(rep):
            sink = jnp.where(qhead == r, sinks[:, g * rep + r:g * rep + r + 1], sink)
        if band:
            band_v = jnp.concatenate([vp_ref[0, ks, :], vc_ref[0, ks, :], vn_ref[0, ks, :]], axis=1)
            v_all = jnp.concatenate([band_v[:, j * t:(j + 3) * t], vx_ref[0, ks, :]], axis=1)
        else:
            v_all = vx_ref[0, ks, :]
        m = jnp.maximum(jnp.max(s, axis=0, keepdims=True), sink)
        p = jnp.exp2(s - m).astype(BF16)
        ov = jnp.dot(jnp.concatenate([v_all, ones], axis=0), p, preferred_element_type=F32)
        inv = 1.0 / (ov[hd:hd + 1] + jnp.exp2(sink - m))
        o = ov[0:hd] * inv
        for r in range(rep):
            o_ref[0, rows, (g * rep + r) * hd:(g * rep + r + 1) * hd] = (
                o[:, r * rows_per:(r + 1) * rows_per].T.astype(BF16))

    chains = [(g, j) for g in range(ATTN_KV_HEADS) for j in range(nsub)]
    s_next = scores(*chains[0])
    for n, (g, j) in enumerate(chains):
        s_cur = s_next
        if n + 1 < len(chains):
            s_next = scores(*chains[n + 1])
        finish(g, j, s_cur)


def _attention(q, k, vt, kx, vtx, vecs, *, layer, band):
    b, l, _ = q.shape
    t = ATTN_BLOCK
    tq = min(l, 4 * t)
    nsub = tq // t
    nblk = l // t
    lc = kx.shape[1] // b
    qs = pl.BlockSpec((1, tq, D_ATTN), lambda bi, i: (bi, i, 0))
    kctx = pl.BlockSpec((1, lc, KV_DIM), lambda bi, i: (0, bi, 0))
    vctx = pl.BlockSpec((1, KV_DIM, lc), lambda bi, i: (0, 0, bi))
    vs = _vec_spec(vecs, layer)
    if band:
        prv = lambda i: jnp.maximum(i * nsub - 1, 0)
        nxt = lambda i: jnp.minimum((i + 1) * nsub, nblk - 1)
        kspec = lambda rows, f: pl.BlockSpec((1, rows, KV_DIM), lambda bi, i: (bi, f(i), 0))
        vspec = lambda rows, f: pl.BlockSpec((1, KV_DIM, rows), lambda bi, i: (bi, 0, f(i)))
        cur = lambda i: i
        in_specs = [qs, kspec(t, prv), kspec(tq, cur), kspec(t, nxt),
                    vspec(t, prv), vspec(tq, cur), vspec(t, nxt), kctx, vctx, vs]
        args = (q, k, k, k, vt, vt, vt, kx, vtx, vecs)
    else:
        in_specs = [qs, kctx, vctx, vs]
        args = (q, kx, vtx, vecs)
    return pl.pallas_call(
        functools.partial(_attn_kernel, band=band),
        grid=(b, l // tq),
        in_specs=in_specs,
        out_specs=qs,
        out_shape=jax.ShapeDtypeStruct((b, l, D_ATTN), BF16),
        compiler_params=_cparams(("arbitrary", "arbitrary")),
        name="attn_band" if band else "attn_ctx",
    )(*args)


def _outproj_kernel(yf_ref, yb_ref, xs_ref, z_ref, ya_ref, x_ref, w_ref, v_ref, gate_ref, scf_ref, shf_ref,
                    xo_ref, hm_ref, *, sub, cond_row):
    d_skip = _vec(v_ref, V_DSKIP, D_SSM)
    g_ssm = _vec(v_ref, V_SSM_NORM, D_SSM)
    gate = _cond(gate_ref, cond_row) * _vec(v_ref, V_G_POST_MIX)
    g_pre = _vec(v_ref, V_G_PRE_MLP) * (1.0 + _cond(scf_ref, cond_row))
    shift = _cond(shf_ref, cond_row)
    for r in range(x_ref.shape[1] // sub):
        rows = slice(r * sub, (r + 1) * sub)
        y = (yf_ref[0, rows, :].astype(F32) + yb_ref[0, rows, :].astype(F32)
             + xs_ref[0, rows, :].astype(F32) * d_skip)
        y = y * _silu(z_ref[0, rows, :].astype(F32))
        cat = jnp.concatenate([_rms(y, g_ssm).astype(BF16), ya_ref[0, rows, :]], axis=1)
        mix = jnp.dot(cat, w_ref[0], preferred_element_type=F32)
        xn = x_ref[0, rows, :] + _unit(mix) * gate
        xo_ref[0, rows, :] = xn
        hm_ref[0, rows, :] = (_unit(xn) * g_pre + shift).astype(BF16)


def _outproj(yf, yb, xbc, z, ya, x, w_out_b, vecs, mods, *, layer, cond_row, tm):
    b, l, d = x.shape
    row = lambda w: pl.BlockSpec((1, tm, w), lambda bi, i: (bi, i, 0))
    return pl.pallas_call(
        functools.partial(_outproj_kernel, sub=min(tm, 256), cond_row=cond_row),
        grid=(b, l // tm),
        in_specs=[row(D_SSM), row(D_SSM), row(D_SSM), row(D_SSM), row(D_ATTN), row(d),
                  _const_spec((1,) + w_out_b.shape[1:], (layer, 0, 0)), _vec_spec(vecs, layer),
                  _mod_spec(mods, layer, M_G_MIX), _mod_spec(mods, layer, M_SC_MLP),
                  _mod_spec(mods, layer, M_SH_MLP)],
        out_specs=[row(d), row(d)],
        out_shape=[jax.ShapeDtypeStruct((b, l, d), F32), jax.ShapeDtypeStruct((b, l, d), BF16)],
        compiler_params=_cparams(("arbitrary", "arbitrary")),
        name="outproj",
    )(yf, yb, xbc, z, ya, x, w_out_b, vecs, mods, mods, mods)


def _mlp_kernel(h_ref, x_ref, w1_ref, w2_ref, v_ref, gate_ref, xo_ref, acc_ref, *, cond_row):
    k = pl.program_id(2)

    @pl.when(k == 0)
    def _():
        acc_ref[...] = jnp.zeros_like(acc_ref)

    a = jnp.dot(h_ref[0], w1_ref[0], preferred_element_type=F32)
    a = jnp.square(jnp.maximum(a, 0.0)).astype(BF16)
    acc_ref[...] += jnp.dot(a, w2_ref[0], preferred_element_type=F32)

    @pl.when(k == pl.num_programs(2) - 1)
    def _():
        xo_ref[0] = x_ref[0] + _cond(gate_ref, cond_row) * _rms(acc_ref[...], _vec(v_ref, V_G_POST_MLP))


def _mlp(h, x, w1_b, w2_b, vecs, mods, *, layer, cond_row, tm, tf):
    b, l, d = x.shape
    dff = w1_b.shape[2]
    row = pl.BlockSpec((1, tm, d), lambda bi, i, k: (bi, i, 0))
    return pl.pallas_call(
        functools.partial(_mlp_kernel, cond_row=cond_row),
        grid=(b, l // tm, dff // tf),
        in_specs=[row, row,
                  pl.BlockSpec((1, d, tf), lambda bi, i, k: (layer, 0, k)),
                  pl.BlockSpec((1, tf, d), lambda bi, i, k: (layer, k, 0)),
                  _vec_spec(vecs, layer), _mod_spec(mods, layer, M_G_MLP)],
        out_specs=row,
        out_shape=jax.ShapeDtypeStruct((b, l, d), F32),
        scratch_shapes=[pltpu.VMEM((tm, d), F32)],
        compiler_params=_cparams(("arbitrary", "arbitrary", "arbitrary")),
        name="mlp",
    )(h, x, w1_b, w2_b, vecs, mods)


def _rope_tables(l):
    n_freq = ATTN_HEAD_DIM // 4
    inv_freq = ROPE_BASE ** (-np.arange(n_freq, dtype=np.float64) / n_freq)
    t = np.arange(l)
    ang_r = (t // GRID_W)[:, None] * inv_freq[None, :]
    ang_c = (t % GRID_W)[:, None] * inv_freq[None, :]
    cos = np.concatenate([np.cos(ang_r)] * 2 + [np.cos(ang_c)] * 2, axis=-1)
    sin = np.concatenate([-np.sin(ang_r), np.sin(ang_r), -np.sin(ang_c), np.sin(ang_c)], axis=-1)
    return jnp.asarray(cos, F32), jnp.asarray(sin, F32)


def _pack_vectors(d, g_pre_mix, g_post_mix, g_pre_mlp, g_post_mlp, ssm_norm, d_skip, conv_b, dt_bias, a_log,
                  attn_sink):
    depth = g_pre_mix.shape[0]

    def r(v):
        v = v.reshape(depth, 1, -1).astype(F32)
        return jnp.pad(v, ((0, 0), (0, 0), (0, d - v.shape[-1])))

    rows = [None] * VEC_ROWS
    rows[V_G_PRE_MIX], rows[V_G_POST_MIX] = r(g_pre_mix), r(g_post_mix)
    rows[V_G_PRE_MLP], rows[V_G_POST_MLP] = r(g_pre_mlp), r(g_post_mlp)
    rows[V_SSM_NORM], rows[V_DSKIP] = r(ssm_norm), r(jnp.repeat(d_skip, SSM_HEAD_DIM, axis=-1))
    rows[V_CONV_B], rows[V_DT_BIAS], rows[V_A_LOG], rows[V_SINK] = r(conv_b), r(dt_bias), r(a_log), r(attn_sink)
    zero = jnp.zeros((depth, 1, d), F32)
    return jnp.concatenate([zero if v is None else v for v in rows], axis=1)


def kernel(x, c, ctx, c_ctx, w_mod, b_mod, g_pre_mix, g_post_mix, g_pre_mlp, g_post_mlp, w_in, conv_w,
           conv_b, a_log, dt_bias, d_skip, ssm_norm, attn_sink, w_out, w_ff1, w_ff2):
    b, l, d = x.shape
    lc = ctx.shape[1]
    depth = w_mod.shape[0]
    assert b <= 7 and l % ROW_TILE == 0 and lc % SSM_CHUNK == 0 and w_in.shape[2] == D_IN_PROJ

    cond = jnp.concatenate([c, c_ctx[None, :], jnp.zeros((8 - b - 1, d), F32)], axis=0)
    ctx_row = b
    mods = _modulation(cond, w_mod, b_mod)
    vecs = _pack_vectors(d, g_pre_mix, g_post_mix, g_pre_mlp, g_post_mlp, ssm_norm, d_skip, conv_b, dt_bias,
                         a_log, attn_sink)
    w_in_t = jnp.swapaxes(w_in, 1, 2).astype(BF16)
    w_out_b = w_out.astype(BF16)
    w1_b = w_ff1.astype(BF16)
    w2_b = w_ff2.astype(BF16)
    tables = _rope_tables(l)
    h_zero = jnp.zeros((b, SSM_STATE, D_SSM), F32)

    nc = b * lc
    tm_c = min(nc, ROW_TILE)
    flat = lambda a: a.reshape(1, nc, a.shape[-1])
    per_batch = lambda a: a.reshape(b, lc, a.shape[-1])
    xc = flat(ctx)
    for i in range(depth):
        last = i == depth - 1
        xbc_c, z_c, q_c, k_c, vt_c, dt_c = _inproj(xc, vecs, mods, w_in_t, None, layer=i, cond_row=ctx_row, tm=tm_c)
        xbc_c = _conv_silu(per_batch(xbc_c), conv_w, vecs, layer=i, t=min(lc, ROW_TILE))
        yf_c, yb_c, h_f, h_b = _ssd(xbc_c, per_batch(dt_c), vecs, h_zero, h_zero, layer=i)

        xbc_l, z_l, q_l, k_l, vt_l, dt_l = _inproj(x, vecs, mods, w_in_t, tables, layer=i, cond_row=None,
                                                   tm=ROW_TILE)
        xbc_l = _conv_silu(xbc_l, conv_w, vecs, layer=i, t=ROW_TILE)
        yf, yb, _, _ = _ssd(xbc_l, dt_l, vecs, h_f, h_b, layer=i)
        ya = _attention(q_l, k_l, vt_l, k_c, vt_c, vecs, layer=i, band=True)
        x, hm = _outproj(yf, yb, xbc_l, z_l, ya, x, w_out_b, vecs, mods, layer=i, cond_row=None, tm=ROW_TILE)
        x = _mlp(hm, x, w1_b, w2_b, vecs, mods, layer=i, cond_row=None, tm=ROW_TILE, tf=FF_TILE)

        if not last:
            ya_c = _attention(per_batch(q_c), None, None, k_c, vt_c, vecs, layer=i, band=False)
            xc, hm_c = _outproj(flat(yf_c), flat(yb_c), flat(xbc_c), z_c, flat(ya_c), xc, w_out_b, vecs, mods,
                                layer=i, cond_row=ctx_row, tm=tm_c)
            xc = _mlp(hm_c, xc, w1_b, w2_b, vecs, mods, layer=i, cond_row=ctx_row, tm=tm_c, tf=FF_TILE)
    return x
```

```python
import functools

import numpy as np
import jax
import jax.numpy as jnp
from jax import lax
from jax.experimental import pallas as pl
from jax.experimental.pallas import tpu as pltpu

F32 = jnp.float32
BF16 = jnp.bfloat16

N_MOD = 6
SSM_HEAD_DIM = 64
SSM_HEADS = 16
SSM_STATE = 128
SSM_GROUPS = 2
SSM_CONV = 5
SSM_CHUNK = 128
ATTN_HEAD_DIM = 128
ATTN_HEADS = 8
ATTN_KV_HEADS = 2
ATTN_BLOCK = 128
GRID_W = 64
ROPE_BASE = 10000.0
EPS = 1e-6
LOG2_E = 1.4426950408889634

D_SSM = SSM_HEADS * SSM_HEAD_DIM
SSM_BC = SSM_GROUPS * SSM_STATE
D_XBC = D_SSM + 2 * SSM_BC
D_ATTN = ATTN_HEADS * ATTN_HEAD_DIM
KV_DIM = ATTN_KV_HEADS * ATTN_HEAD_DIM
N_DT = 2 * SSM_HEADS
D_IN_PROJ = D_XBC + D_SSM + N_DT + D_ATTN + 2 * KV_DIM
HEADS_PER_GROUP = SSM_HEADS // SSM_GROUPS
GROUP_COLS = HEADS_PER_GROUP * SSM_HEAD_DIM

LANES = 128
BF16_SUBLANES = 16
VMEM_LIMIT_BYTES = 56 * 1024 * 1024

ROW_TILE = 512
FF_TILE = 1024
CONV_ROWS = 128
SSD_CHUNKS_PER_STEP = 4
CAST_SLAB_BYTES = 2 * 1024 * 1024

V_G_PRE_MIX, V_G_POST_MIX, V_G_PRE_MLP, V_G_POST_MLP = 0, 1, 2, 3
V_SSM_NORM, V_DSKIP, V_CONV_B, V_DT_BIAS, V_A_LOG, V_SINK = 4, 5, 6, 7, 8, 9
VEC_ROWS = 16
M_SH_MIX, M_SC_MIX, M_G_MIX, M_SH_MLP, M_SC_MLP, M_G_MLP = range(N_MOD)


def _cparams(sem):
    return pltpu.CompilerParams(dimension_semantics=sem, vmem_limit_bytes=VMEM_LIMIT_BYTES)


def _unit(xf):
    ms = jnp.mean(xf * xf, axis=-1, keepdims=True)
    return xf * lax.rsqrt(ms + EPS)


def _rms(xf, g):
    return _unit(xf) * g


def _silu(v):
    return v * jax.nn.sigmoid(v)


def _const_spec(block, index):
    return pl.BlockSpec(block, lambda *_: index, pipeline_mode=pl.Buffered(1))


def _vec_spec(vecs, layer):
    return _const_spec((1,) + vecs.shape[1:], (layer, 0, 0))


def _mod_spec(mods, layer, chunk):
    d = mods.shape[-1] // N_MOD
    return _const_spec((1, mods.shape[1], d), (layer, 0, chunk))


def _vec(v_ref, row, width=None):
    return v_ref[0, row:row + 1, :] if width is None else v_ref[0, row:row + 1, 0:width]


def _cond(m_ref, cond_row):
    if cond_row is None:
        return m_ref[0, pl.ds(pl.program_id(0), 1), :]
    return m_ref[0, cond_row:cond_row + 1, :]


def _mod_kernel(c_ref, w_ref, b_ref, o_ref):
    s = _silu(c_ref[...]).astype(BF16)
    o_ref[0] = jnp.dot(s, w_ref[0].astype(BF16), preferred_element_type=F32) + b_ref[0]


def _modulation(cond, w_mod, b_mod):
    depth, d, n = w_mod.shape
    tn = 1024
    return pl.pallas_call(
        _mod_kernel,
        grid=(depth, n // tn),
        in_specs=[pl.BlockSpec((8, d), lambda l, j: (0, 0)),
                  pl.BlockSpec((1, d, tn), lambda l, j: (l, 0, j)),
                  pl.BlockSpec((1, 1, tn), lambda l, j: (l, 0, j))],
        out_specs=pl.BlockSpec((1, 8, tn), lambda l, j: (l, 0, j)),
        out_shape=jax.ShapeDtypeStruct((depth, 8, n), F32),
        compiler_params=_cparams(("arbitrary", "arbitrary")),
        name="modulation",
    )(cond, w_mod, b_mod.reshape(depth, 1, n))


def _rope(v, cos, sin):
    lane = lax.broadcasted_iota(jnp.int32, v.shape, 1)
    partner = jnp.where((lane % 64) < 32, pltpu.roll(v, 96, 1), pltpu.roll(v, 32, 1))
    return v * cos + partner * sin


def _inproj_kernel(*refs, rope, cond_row, n_cast):
    n_in = 7 if rope else 5
    if rope:
        x_ref, v_ref, sc_ref, sh_ref, w_ref, cos_ref, sin_ref = refs[:n_in]
    else:
        x_ref, v_ref, sc_ref, sh_ref, w_ref = refs[:n_in]
    outs = refs[n_in + n_cast:]
    xbc_ref, z_ref, q_ref, k_ref, vt_ref, dt_ref = outs[:6]
    h_scr = outs[-1]
    for src, dst in zip(refs[n_in:n_in + n_cast], outs[6:6 + n_cast]):
        dst[...] = src[...].astype(BF16)
    h = (_rms(x_ref[0], _vec(v_ref, V_G_PRE_MIX)) * (1.0 + _cond(sc_ref, cond_row))
         + _cond(sh_ref, cond_row))
    h_scr[...] = h.astype(BF16)

    nt = (((1,), (1,)), ((), ()))

    def proj(r0, n):
        return lax.dot_general(h_scr[...], w_ref[0, r0:r0 + n, :], nt, preferred_element_type=F32)

    seg = 512
    for s in range(D_XBC // seg):
        xbc_ref[0, :, s * seg:(s + 1) * seg] = proj(s * seg, seg).astype(BF16)
    for s in range(D_SSM // seg):
        z_ref[0, :, s * seg:(s + 1) * seg] = proj(D_XBC + s * seg, seg).astype(BF16)

    dt0 = D_XBC + D_SSM
    lane = lax.broadcasted_iota(jnp.int32, (x_ref.shape[1], LANES), 1)
    dt_ref[0] = jnp.where(lane < N_DT, proj(dt0, LANES), 0.0)

    scale = ATTN_HEAD_DIM ** -0.5 * LOG2_E
    hd = ATTN_HEAD_DIM
    q0 = dt0 + N_DT
    for s in range(D_ATTN // seg):
        acc = proj(q0 + s * seg, seg)
        for hh in range(seg // hd):
            v = acc[:, hh * hd:(hh + 1) * hd]
            if rope:
                v = _rope(v, cos_ref[...], sin_ref[...])
            q_ref[0, :, s * seg + hh * hd:s * seg + (hh + 1) * hd] = (v * scale).astype(BF16)
    acc = proj(q0 + D_ATTN, KV_DIM)
    for hh in range(ATTN_KV_HEADS):
        v = acc[:, hh * hd:(hh + 1) * hd]
        if rope:
            v = _rope(v, cos_ref[...], sin_ref[...])
        k_ref[0, :, hh * hd:(hh + 1) * hd] = v.astype(BF16)
    v0 = q0 + D_ATTN + KV_DIM
    vt_ref[0] = lax.dot_general(w_ref[0, v0:v0 + KV_DIM, :], h_scr[...], nt,
                                preferred_element_type=F32).astype(BF16)


def _cast_slabs(weights, layer, nsteps, steps_per_batch):
    specs, out_specs, out_shape = [], [], []
    for w in weights:
        rows = w.shape[1] // nsteps
        if w.shape[1] % nsteps or rows % BF16_SUBLANES or rows * w.shape[2] * 4 > CAST_SLAB_BYTES:
            return None
        blk = (1, rows, w.shape[2])
        specs.append(pl.BlockSpec(blk, lambda bi, i: (layer, bi * steps_per_batch + i, 0)))
        out_specs.append(pl.BlockSpec(blk, lambda bi, i: (0, bi * steps_per_batch + i, 0)))
        out_shape.append(jax.ShapeDtypeStruct((1,) + w.shape[1:], BF16))
    return specs, out_specs, out_shape


def _inproj(x, vecs, mods, w_in_t, tables, *, layer, cond_row, tm, cast=()):
    b, l, d = x.shape
    rope = tables is not None
    row = lambda w: pl.BlockSpec((1, tm, w), lambda bi, i: (bi, i, 0))
    tab = pl.BlockSpec((tm, LANES), lambda bi, i: (i, 0))
    widths = [(D_XBC, BF16), (D_SSM, BF16), (D_ATTN, BF16), (KV_DIM, BF16)]
    out_specs = [row(w) for w, _ in widths] + [pl.BlockSpec((1, KV_DIM, tm), lambda bi, i: (bi, 0, i)), row(LANES)]
    out_shape = ([jax.ShapeDtypeStruct((b, l, w), dt) for w, dt in widths]
                 + [jax.ShapeDtypeStruct((b, KV_DIM, l), BF16), jax.ShapeDtypeStruct((b, l, LANES), F32)])
    in_specs = [row(d), _vec_spec(vecs, layer), _mod_spec(mods, layer, M_SC_MIX), _mod_spec(mods, layer, M_SH_MIX),
                _const_spec((1,) + w_in_t.shape[1:], (layer, 0, 0))]
    args = [x, vecs, mods, mods, w_in_t]
    if rope:
        in_specs += [tab, tab]
        args += list(tables)
    if cast:
        c_in, c_out, c_shape = _cast_slabs(cast, layer, b * (l // tm), l // tm)
        in_specs += c_in
        args += list(cast)
        out_specs += c_out
        out_shape += c_shape
    return pl.pallas_call(
        functools.partial(_inproj_kernel, rope=rope, cond_row=cond_row, n_cast=len(cast)),
        grid=(b, l // tm),
        in_specs=in_specs,
        out_specs=out_specs,
        out_shape=out_shape,
        scratch_shapes=[pltpu.VMEM((tm, d), BF16)],
        compiler_params=_cparams(("arbitrary", "arbitrary")),
        name="inproj_rope" if rope else "inproj",
    )(*args)


def _conv_kernel(u_ref, prev_ref, next_ref, w_ref, v_ref, o_ref, ext_scr, *, t):
    i = pl.program_id(1)
    n = pl.num_programs(1)
    halo = SSM_CONV // 2
    h16 = BF16_SUBLANES
    zero = jnp.zeros(prev_ref.shape[1:], BF16)
    ext_scr[0:h16, :] = jnp.where(i > 0, prev_ref[0], zero)
    ext_scr[h16:h16 + t, :] = u_ref[0]
    ext_scr[h16 + t:h16 + t + h16, :] = jnp.where(i < n - 1, next_ref[0], zero)
    blk = CONV_ROWS
    win = blk + 2 * h16
    rr = lax.broadcasted_iota(jnp.int32, (blk, SSM_CONV * win), 0)
    cc = lax.broadcasted_iota(jnp.int32, (blk, SSM_CONV * win), 1)
    hit = cc == rr + h16 - halo
    for k in range(1, SSM_CONV):
        hit = hit | (cc == rr + k * win + h16 + k - halo)
    shift_all = hit.astype(F32).astype(BF16)
    taps = w_ref[0].astype(BF16)
    bias = _vec(v_ref, V_CONV_B, D_XBC)
    for r in range(t // blk):
        window = ext_scr[r * blk:r * blk + win, :]
        scaled = jnp.concatenate([window * taps[k:k + 1, :] for k in range(SSM_CONV)], axis=0)
        acc = jnp.dot(shift_all, scaled, preferred_element_type=F32) + bias
        o_ref[0, r * blk:(r + 1) * blk, :] = _silu(acc).astype(BF16)


def _conv_silu(u, conv_w, vecs, *, layer, t):
    b, l, c = u.shape
    h16 = BF16_SUBLANES
    nh = l // h16
    per = t // h16
    return pl.pallas_call(
        functools.partial(_conv_kernel, t=t),
        grid=(b, l // t),
        in_specs=[pl.BlockSpec((1, t, c), lambda bi, i: (bi, i, 0)),
                  pl.BlockSpec((1, h16, c), lambda bi, i: (bi, jnp.maximum(i * per - 1, 0), 0)),
                  pl.BlockSpec((1, h16, c), lambda bi, i: (bi, jnp.minimum((i + 1) * per, nh - 1), 0)),
                  _const_spec((1,) + conv_w.shape[1:], (layer, 0, 0)), _vec_spec(vecs, layer)],
        out_specs=pl.BlockSpec((1, t, c), lambda bi, i: (bi, i, 0)),
        out_shape=jax.ShapeDtypeStruct((b, l, c), BF16),
        scratch_shapes=[pltpu.VMEM((t + 2 * h16, c), BF16)],
        compiler_params=_cparams(("arbitrary", "arbitrary")),
        name="conv_silu",
    )(u, u, u, conv_w, vecs)


def _split3(v):
    hi = v.astype(BF16)
    r1 = v - hi.astype(F32)
    mid = r1.astype(BF16)
    lo = (r1 - mid.astype(F32)).astype(BF16)
    return hi, mid, lo


def _ssd_kernel(xf_ref, xb_ref, dtf_ref, dtb_ref, v_ref, h0f_ref, h0b_ref,
                yf_ref, yb_ref, hff_ref, hfb_ref, ef_scr, eb_scr):
    @pl.when(pl.program_id(1) == 0)
    def _():
        r = lax.broadcasted_iota(jnp.int32, ef_scr.shape, 0) % LANES
        col = lax.broadcasted_iota(jnp.int32, ef_scr.shape, 1)
        for h0_ref, hf_ref, e_scr, off in ((h0f_ref, hff_ref, ef_scr, 0), (h0b_ref, hfb_ref, eb_scr, SSM_HEADS)):
            hf_ref[0] = h0_ref[0]
            e_scr[...] = (r == col // SSM_HEAD_DIM + off).astype(BF16)

    bias = _vec(v_ref, V_DT_BIAS, LANES)
    a_log = _vec(v_ref, V_A_LOG, LANES)
    q = SSM_CHUNK
    n = xf_ref.shape[1] // q
    order = [(j, n - 1 - j) for j in range(n)]
    preps = [(_ssd_prep(dtf_ref[0, jf * q:(jf + 1) * q, :], bias, a_log, ef_scr, rev=False),
              _ssd_prep(dtb_ref[0, jb * q:(jb + 1) * q, :], bias, a_log, eb_scr, rev=True)) for jf, jb in order]
    for (jf, jb), (prep_f, prep_b) in zip(order, preps):
        _ssd_chunk(xf_ref, yf_ref, hff_ref, prep_f, slice(jf * q, (jf + 1) * q), rev=False)
        _ssd_chunk(xb_ref, yb_ref, hfb_ref, prep_b, slice(jb * q, (jb + 1) * q), rev=True)


def _ssd_prep(dt_raw, bias, a_log, e_scr, *, rev):
    q = SSM_CHUNK
    x = dt_raw + bias
    dt = jnp.maximum(x, 0.0) + jnp.log1p(jnp.exp(-jnp.abs(x)))
    a2 = -jnp.exp(a_log) * LOG2_E
    dta = dt * a2

    ri = lax.broadcasted_iota(jnp.int32, (q, q), 0)
    ci = lax.broadcasted_iota(jnp.int32, (q, q), 1)
    mask = (ci >= ri) if rev else (ci <= ri)
    tri = mask.astype(BF16)
    cs = sum(jnp.dot(tri, part, preferred_element_type=F32) for part in _split3(dta))
    tot = cs[0:1] if rev else cs[q - 1:q]
    row_t = (cs - jnp.log2(dt)).T

    ecs = jnp.exp2(cs)
    ecs_hi = ecs.astype(BF16)
    ecs_lo = (ecs - ecs_hi.astype(F32)).astype(BF16)
    w_end = (dt * jnp.exp2(tot - cs)).astype(BF16)
    wx = jnp.dot(w_end, e_scr[0:LANES, :], preferred_element_type=F32)
    ecsx = jnp.dot(jnp.concatenate([ecs_hi, ecs_lo], axis=1), e_scr[...], preferred_element_type=F32)
    return dict(mask=mask, cs=cs, row_t=row_t, wx=wx, ecsx=ecsx)


def _ssd_chunk(xbc_ref, y_ref, hf_ref, prep, rows, *, rev):
    q = SSM_CHUNK
    off = SSM_HEADS if rev else 0
    mask, cs, row_t, wx, ecsx = prep["mask"], prep["cs"], prep["row_t"], prep["wx"], prep["ecsx"]
    cdec = ecsx[0:1] if rev else ecsx[q - 1:q]

    lane_lo = lax.broadcasted_iota(jnp.int32, (q, LANES), 1) < SSM_HEAD_DIM
    nt = (((1,), (1,)), ((), ()))
    for g in range(SSM_GROUPS):
        gc = slice(g * GROUP_COLS, (g + 1) * GROUP_COLS)
        bg = xbc_ref[0, rows, D_SSM + g * SSM_STATE:D_SSM + (g + 1) * SSM_STATE]
        cg = xbc_ref[0, rows, D_SSM + SSM_BC + g * SSM_STATE:D_SSM + SSM_BC + (g + 1) * SSM_STATE]
        cb = lax.dot_general(cg, bg, nt, preferred_element_type=F32)
        s_in = hf_ref[0, :, gc]
        y_off = jnp.dot(cg, s_in.astype(BF16), preferred_element_type=F32) * ecsx[:, gc]
        xg = xbc_ref[0, rows, gc]
        xw = (xg.astype(F32) * wx[:, gc]).astype(BF16)
        bg_t = bg.astype(F32).T.astype(BF16)
        hf_ref[0, :, gc] = s_in * cdec[:, gc] + jnp.dot(bg_t, xw, preferred_element_type=F32)

        for jp in range(HEADS_PER_GROUP // 2):
            pair = g * (HEADS_PER_GROUP // 2) + jp

            def head_mat(h):
                seg = cs[:, off + h:off + h + 1] - row_t[off + h:off + h + 1, :]
                return (cb * jnp.exp2(jnp.where(mask, seg, -jnp.inf))).astype(BF16)

            lhs = jnp.concatenate([head_mat(2 * pair), head_mat(2 * pair + 1)], axis=1)
            xp = xbc_ref[0, rows, pair * LANES:(pair + 1) * LANES]
            zero = jnp.zeros_like(xp)
            rhs = jnp.concatenate([jnp.where(lane_lo, xp, zero), jnp.where(lane_lo, zero, xp)], axis=0)
            y_d = jnp.dot(lhs, rhs, preferred_element_type=F32)
            y_ref[0, rows, pair * LANES:(pair + 1) * LANES] = (
                y_d + y_off[:, jp * LANES:(jp + 1) * LANES]).astype(BF16)


def _ssd(xbc, dt_raw, vecs, h0_f, h0_b, *, layer):
    b, l, _ = xbc.shape
    q = min(l, SSM_CHUNK * SSD_CHUNKS_PER_STEP)
    nc = l // q
    fwd = lambda w: pl.BlockSpec((1, q, w), lambda bi, c: (bi, c, 0))
    bwd = lambda w: pl.BlockSpec((1, q, w), lambda bi, c: (bi, nc - 1 - c, 0))
    st = pl.BlockSpec((1, SSM_STATE, D_SSM), lambda bi, c: (bi, 0, 0))
    y_shape = jax.ShapeDtypeStruct((b, l, D_SSM), BF16)
    h_shape = jax.ShapeDtypeStruct((b, SSM_STATE, D_SSM), F32)
    return pl.pallas_call(
        _ssd_kernel,
        grid=(b, nc),
        in_specs=[fwd(D_XBC), bwd(D_XBC), fwd(LANES), bwd(LANES), _vec_spec(vecs, layer), st, st],
        out_specs=[fwd(D_SSM), bwd(D_SSM), st, st],
        out_shape=[y_shape, y_shape, h_shape, h_shape],
        scratch_shapes=[pltpu.VMEM((2 * LANES, D_SSM), BF16), pltpu.VMEM((2 * LANES, D_SSM), BF16)],
        compiler_params=_cparams(("arbitrary", "arbitrary")),
        name="ssd",
    )(xbc, xbc, dt_raw, dt_raw, vecs, h0_f, h0_b)


def _attn_kernel(*refs, band):
    if band:
        q_ref, kp_ref, kc_ref, kn_ref, vp_ref, vc_ref, vn_ref, kx_ref, vx_ref, v_ref, o_ref = refs
    else:
        q_ref, kx_ref, vx_ref, v_ref, o_ref = refs
    i = pl.program_id(1)
    t = ATTN_BLOCK
    hd = ATTN_HEAD_DIM
    rep = ATTN_HEADS // ATTN_KV_HEADS
    tq = q_ref.shape[1]
    nsub = tq // t if band else 1
    rows_per = t if band else tq
    nblk = pl.num_programs(1) * nsub
    lc = kx_ref.shape[1]
    nk = (3 * t if band else 0) + lc
    nq = rep * rows_per
    nt = (((1,), (1,)), ((), ()))

    if band:
        krow = lax.broadcasted_iota(jnp.int32, (t, nq), 0)
        qcol = lax.broadcasted_iota(jnp.int32, (t, nq), 1) % t
    qhead = lax.broadcasted_iota(jnp.int32, (1, nq), 1) // rows_per
    ones = jnp.ones((BF16_SUBLANES, nk), BF16)
    sinks = _vec(v_ref, V_SINK, LANES) * LOG2_E

    def scores(g, j):
        ks = slice(g * hd, (g + 1) * hd)
        rows = slice(j * rows_per, (j + 1) * rows_per)
        qg = jnp.concatenate([q_ref[0, rows, (g * rep + r) * hd:(g * rep + r + 1) * hd] for r in range(rep)],
                             axis=0)
        if band:
            band_k = jnp.concatenate([kp_ref[0, :, ks], kc_ref[0, :, ks], kn_ref[0, :, ks]], axis=0)
            k_all = jnp.concatenate([band_k[j * t:(j + 3) * t], kx_ref[0, :, ks]], axis=0)
        else:
            k_all = kx_ref[0, :, ks]
        s = lax.dot_general(k_all, qg, nt, preferred_element_type=F32)
        if band:
            qb = i * nsub + j
            lo = qcol + jnp.where(qb > 0, 0, 2 * t)
            hi = qcol - jnp.where(qb < nblk - 1, 0, 2 * t)
            s = jnp.concatenate([jnp.where(krow >= lo, s[0:t], -jnp.inf), s[t:2 * t],
                                 jnp.where(krow <= hi, s[2 * t:3 * t], -jnp.inf), s[3 * t:]], axis=0)
        return s

    def finish(g, j, s):
        ks = slice(g * hd, (g + 1) * hd)
        rows = slice(j * rows_per, (j + 1) * rows_per)
        sink = jnp.zeros((1, nq), F32)
        for r in range(rep):
            sink = jnp.where(qhead == r, sinks[:, g * rep + r:g * rep + r + 1], sink)
        if band:
            band_v = jnp.concatenate([vp_ref[0, ks, :], vc_ref[0, ks, :], vn_ref[0, ks, :]], axis=1)
            v_all = jnp.concatenate([band_v[:, j * t:(j + 3) * t], vx_ref[0, ks, :]], axis=1)
        else:
            v_all = vx_ref[0, ks, :]
        m = jnp.maximum(jnp.max(s, axis=0, keepdims=True), sink)
        p = jnp.exp2(s - m).astype(BF16)
        ov = jnp.dot(jnp.concatenate([v_all, ones], axis=0), p, preferred_element_type=F32)
        inv = 1.0 / (ov[hd:hd + 1] + jnp.exp2(sink - m))
        o = ov[0:hd] * inv
        for r in range(rep):
            o_ref[0, rows, (g * rep + r) * hd:(g * rep + r + 1) * hd] = (
                o[:, r * rows_per:(r + 1) * rows_per].T.astype(BF16))

    chains = [(g, j) for g in range(ATTN_KV_HEADS) for j in range(nsub)]
    s_next = scores(*chains[0])
    for n, (g, j) in enumerate(chains):
        s_cur = s_next
        if n + 1 < len(chains):
            s_next = scores(*chains[n + 1])
        finish(g, j, s_cur)


def _attention(q, k, vt, kx, vtx, vecs, *, layer, band):
    b, l, _ = q.shape
    t = ATTN_BLOCK
    tq = min(l, 4 * t)
    nsub = tq // t
    nblk = l // t
    lc = kx.shape[1] // b
    qs = pl.BlockSpec((1, tq, D_ATTN), lambda bi, i: (bi, i, 0))
    kctx = pl.BlockSpec((1, lc, KV_DIM), lambda bi, i: (0, bi, 0))
    vctx = pl.BlockSpec((1, KV_DIM, lc), lambda bi, i: (0, 0, bi))
    vs = _vec_spec(vecs, layer)
    if band:
        prv = lambda i: jnp.maximum(i * nsub - 1, 0)
        nxt = lambda i: jnp.minimum((i + 1) * nsub, nblk - 1)
        kspec = lambda rows, f: pl.BlockSpec((1, rows, KV_DIM), lambda bi, i: (bi, f(i), 0))
        vspec = lambda rows, f: pl.BlockSpec((1, KV_DIM, rows), lambda bi, i: (bi, 0, f(i)))
        cur = lambda i: i
        in_specs = [qs, kspec(t, prv), kspec(tq, cur), kspec(t, nxt),
                    vspec(t, prv), vspec(tq, cur), vspec(t, nxt), kctx, vctx, vs]
        args = (q, k, k, k, vt, vt, vt, kx, vtx, vecs)
    else:
        in_specs = [qs, kctx, vctx, vs]
        args = (q, kx, vtx, vecs)
    return pl.pallas_call(
        functools.partial(_attn_kernel, band=band),
        grid=(b, l // tq),
        in_specs=in_specs,
        out_specs=qs,
        out_shape=jax.ShapeDtypeStruct((b, l, D_ATTN), BF16),
        compiler_params=_cparams(("arbitrary", "arbitrary")),
        name="attn_band" if band else "attn_ctx",
    )(*args)


def _outproj_kernel(yf_ref, yb_ref, xs_ref, z_ref, ya_ref, x_ref, w_ref, v_ref, gate_ref, scf_ref, shf_ref,
                    xo_ref, hm_ref, *, sub, cond_row):
    d_skip = _vec(v_ref, V_DSKIP, D_SSM)
    g_ssm = _vec(v_ref, V_SSM_NORM, D_SSM)
    gate = _cond(gate_ref, cond_row) * _vec(v_ref, V_G_POST_MIX)
    g_pre = _vec(v_ref, V_G_PRE_MLP) * (1.0 + _cond(scf_ref, cond_row))
    shift = _cond(shf_ref, cond_row)
    for r in range(x_ref.shape[1] // sub):
        rows = slice(r * sub, (r + 1) * sub)
        y = (yf_ref[0, rows, :].astype(F32) + yb_ref[0, rows, :].astype(F32)
             + xs_ref[0, rows, :].astype(F32) * d_skip)
        y = y * _silu(z_ref[0, rows, :].astype(F32))
        cat = jnp.concatenate([_rms(y, g_ssm).astype(BF16), ya_ref[0, rows, :]], axis=1)
        mix = jnp.dot(cat, w_ref[0], preferred_element_type=F32)
        xn = x_ref[0, rows, :] + _unit(mix) * gate
        xo_ref[0, rows, :] = xn
        hm_ref[0, rows, :] = (_unit(xn) * g_pre + shift).astype(BF16)


def _outproj(yf, yb, xbc, z, ya, x, w_out_b, vecs, mods, *, layer, cond_row, tm):
    b, l, d = x.shape
    row = lambda w: pl.BlockSpec((1, tm, w), lambda bi, i: (bi, i, 0))
    return pl.pallas_call(
        functools.partial(_outproj_kernel, sub=min(tm, 256), cond_row=cond_row),
        grid=(b, l // tm),
        in_specs=[row(D_SSM), row(D_SSM), row(D_SSM), row(D_SSM), row(D_ATTN), row(d),
                  _const_spec((1,) + w_out_b.shape[1:], (layer, 0, 0)), _vec_spec(vecs, layer),
                  _mod_spec(mods, layer, M_G_MIX), _mod_spec(mods, layer, M_SC_MLP),
                  _mod_spec(mods, layer, M_SH_MLP)],
        out_specs=[row(d), row(d)],
        out_shape=[jax.ShapeDtypeStruct((b, l, d), F32), jax.ShapeDtypeStruct((b, l, d), BF16)],
        compiler_params=_cparams(("arbitrary", "arbitrary")),
        name="outproj",
    )(yf, yb, xbc, z, ya, x, w_out_b, vecs, mods, mods, mods)


def _mlp_kernel(h_ref, x_ref, w1_ref, w2_ref, v_ref, gate_ref, xo_ref, acc_ref, *, cond_row):
    k = pl.program_id(2)

    @pl.when(k == 0)
    def _():
        acc_ref[...] = jnp.zeros_like(acc_ref)

    a = jnp.dot(h_ref[0], w1_ref[0], preferred_element_type=F32)
    a = jnp.square(jnp.maximum(a, 0.0)).astype(BF16)
    acc_ref[...] += jnp.dot(a, w2_ref[0], preferred_element_type=F32)

    @pl.when(k == pl.num_programs(2) - 1)
    def _():
        xo_ref[0] = x_ref[0] + _cond(gate_ref, cond_row) * _rms(acc_ref[...], _vec(v_ref, V_G_POST_MLP))


def _mlp(h, x, w1_b, w2_b, vecs, mods, *, layer, w_layer, cond_row, tm, tf):
    b, l, d = x.shape
    dff = w1_b.shape[2]
    row = pl.BlockSpec((1, tm, d), lambda bi, i, k: (bi, i, 0))
    return pl.pallas_call(
        functools.partial(_mlp_kernel, cond_row=cond_row),
        grid=(b, l // tm, dff // tf),
        in_specs=[row, row,
                  pl.BlockSpec((1, d, tf), lambda bi, i, k: (w_layer, 0, k)),
                  pl.BlockSpec((1, tf, d), lambda bi, i, k: (w_layer, k, 0)),
                  _vec_spec(vecs, layer), _mod_spec(mods, layer, M_G_MLP)],
        out_specs=row,
        out_shape=jax.ShapeDtypeStruct((b, l, d), F32),
        scratch_shapes=[pltpu.VMEM((tm, d), F32)],
        compiler_params=_cparams(("arbitrary", "arbitrary", "arbitrary")),
        name="mlp",
    )(h, x, w1_b, w2_b, vecs, mods)


def _rope_tables(l):
    n_freq = ATTN_HEAD_DIM // 4
    inv_freq = ROPE_BASE ** (-np.arange(n_freq, dtype=np.float64) / n_freq)
    t = np.arange(l)
    ang_r = (t // GRID_W)[:, None] * inv_freq[None, :]
    ang_c = (t % GRID_W)[:, None] * inv_freq[None, :]
    cos = np.concatenate([np.cos(ang_r)] * 2 + [np.cos(ang_c)] * 2, axis=-1)
    sin = np.concatenate([-np.sin(ang_r), np.sin(ang_r), -np.sin(ang_c), np.sin(ang_c)], axis=-1)
    return jnp.asarray(cos, F32), jnp.asarray(sin, F32)


def _pack_vectors(d, g_pre_mix, g_post_mix, g_pre_mlp, g_post_mlp, ssm_norm, d_skip, conv_b, dt_bias, a_log,
                  attn_sink):
    depth = g_pre_mix.shape[0]

    def r(v):
        v = v.reshape(depth, 1, -1).astype(F32)
        return jnp.pad(v, ((0, 0), (0, 0), (0, d - v.shape[-1])))

    rows = [None] * VEC_ROWS
    rows[V_G_PRE_MIX], rows[V_G_POST_MIX] = r(g_pre_mix), r(g_post_mix)
    rows[V_G_PRE_MLP], rows[V_G_POST_MLP] = r(g_pre_mlp), r(g_post_mlp)
    rows[V_SSM_NORM], rows[V_DSKIP] = r(ssm_norm), r(jnp.repeat(d_skip, SSM_HEAD_DIM, axis=-1))
    rows[V_CONV_B], rows[V_DT_BIAS], rows[V_A_LOG], rows[V_SINK] = r(conv_b), r(dt_bias), r(a_log), r(attn_sink)
    zero = jnp.zeros((depth, 1, d), F32)
    return jnp.concatenate([zero if v is None else v for v in rows], axis=1)


def kernel(x, c, ctx, c_ctx, w_mod, b_mod, g_pre_mix, g_post_mix, g_pre_mlp, g_post_mlp, w_in, conv_w,
           conv_b, a_log, dt_bias, d_skip, ssm_norm, attn_sink, w_out, w_ff1, w_ff2):
    b, l, d = x.shape
    lc = ctx.shape[1]
    depth = w_mod.shape[0]
    assert b <= 7 and l % ROW_TILE == 0 and lc % SSM_CHUNK == 0 and w_in.shape[2] == D_IN_PROJ

    cond = jnp.concatenate([c, c_ctx[None, :], jnp.zeros((8 - b - 1, d), F32)], axis=0)
    ctx_row = b
    mods = _modulation(cond, w_mod, b_mod)
    vecs = _pack_vectors(d, g_pre_mix, g_post_mix, g_pre_mlp, g_post_mlp, ssm_norm, d_skip, conv_b, dt_bias,
                         a_log, attn_sink)
    w_in_t = jnp.swapaxes(w_in, 1, 2).astype(BF16)
    w_out_b = w_out.astype(BF16)
    steps = b * (l // ROW_TILE)
    side_cast = _cast_slabs((w_ff1, w_ff2), 0, steps, l // ROW_TILE) is not None
    if not side_cast:
        w1_b, w2_b = w_ff1.astype(BF16), w_ff2.astype(BF16)
    tables = _rope_tables(l)
    h_zero = jnp.zeros((b, SSM_STATE, D_SSM), F32)

    nc = b * lc
    tm_c = min(nc, ROW_TILE)
    flat = lambda a: a.reshape(1, nc, a.shape[-1])
    per_batch = lambda a: a.reshape(b, lc, a.shape[-1])
    xc = flat(ctx)
    for i in range(depth):
        last = i == depth - 1
        xbc_c, z_c, q_c, k_c, vt_c, dt_c = _inproj(xc, vecs, mods, w_in_t, None, layer=i, cond_row=ctx_row, tm=tm_c)
        xbc_c = _conv_silu(per_batch(xbc_c), conv_w, vecs, layer=i, t=min(lc, ROW_TILE))
        yf_c, yb_c, h_f, h_b = _ssd(xbc_c, per_batch(dt_c), vecs, h_zero, h_zero, layer=i)

        outs = _inproj(x, vecs, mods, w_in_t, tables, layer=i, cond_row=None, tm=ROW_TILE,
                       cast=(w_ff1, w_ff2) if side_cast else ())
        xbc_l, z_l, q_l, k_l, vt_l, dt_l = outs[:6]
        w1_i, w2_i, w_layer = (outs[6], outs[7], 0) if side_cast else (w1_b, w2_b, i)
        xbc_l = _conv_silu(xbc_l, conv_w, vecs, layer=i, t=ROW_TILE)
        yf, yb, _, _ = _ssd(xbc_l, dt_l, vecs, h_f, h_b, layer=i)
        ya = _attention(q_l, k_l, vt_l, k_c, vt_c, vecs, layer=i, band=True)
        x, hm = _outproj(yf, yb, xbc_l, z_l, ya, x, w_out_b, vecs, mods, layer=i, cond_row=None, tm=ROW_TILE)
        x = _mlp(hm, x, w1_i, w2_i, vecs, mods, layer=i, w_layer=w_layer, cond_row=None, tm=ROW_TILE, tf=FF_TILE)

        if not last:
            ya_c = _attention(per_batch(q_c), None, None, k_c, vt_c, vecs, layer=i, band=False)
            xc, hm_c = _outproj(flat(yf_c), flat(yb_c), flat(xbc_c), z_c, flat(ya_c), xc, w_out_b, vecs, mods,
                                layer=i, cond_row=ctx_row, tm=tm_c)
            xc = _mlp(hm_c, xc, w1_i, w2_i, vecs, mods, layer=i, w_layer=w_layer, cond_row=ctx_row, tm=tm_c,
                      tf=FF_TILE)
    return x
```

```python
import functools

import numpy as np
import jax
import jax.numpy as jnp
from jax import lax
from jax.experimental import pallas as pl
from jax.experimental.pallas import tpu as pltpu

F32 = jnp.float32
BF16 = jnp.bfloat16

N_MOD = 6
SSM_HEAD_DIM = 64
SSM_HEADS = 16
SSM_STATE = 128
SSM_GROUPS = 2
SSM_CONV = 5
SSM_CHUNK = 128
ATTN_HEAD_DIM = 128
ATTN_HEADS = 8
ATTN_KV_HEADS = 2
ATTN_BLOCK = 128
GRID_W = 64
ROPE_BASE = 10000.0
EPS = 1e-6
LOG2_E = 1.4426950408889634

D_SSM = SSM_HEADS * SSM_HEAD_DIM
SSM_BC = SSM_GROUPS * SSM_STATE
D_XBC = D_SSM + 2 * SSM_BC
D_ATTN = ATTN_HEADS * ATTN_HEAD_DIM
KV_DIM = ATTN_KV_HEADS * ATTN_HEAD_DIM
N_DT = 2 * SSM_HEADS
D_IN_PROJ = D_XBC + D_SSM + N_DT + D_ATTN + 2 * KV_DIM
HEADS_PER_GROUP = SSM_HEADS // SSM_GROUPS
GROUP_COLS = HEADS_PER_GROUP * SSM_HEAD_DIM

LANES = 128
BF16_SUBLANES = 16
VMEM_LIMIT_BYTES = 56 * 1024 * 1024

ROW_TILE = 512
FF_TILE = 1024
CONV_ROWS = 128
SSD_CHUNKS_PER_STEP = 4
CAST_SLAB_BYTES = 2 * 1024 * 1024

V_G_PRE_MIX, V_G_POST_MIX, V_G_PRE_MLP, V_G_POST_MLP = 0, 1, 2, 3
V_SSM_NORM, V_DSKIP, V_CONV_B, V_DT_BIAS, V_A_LOG, V_SINK = 4, 5, 6, 7, 8, 9
VEC_ROWS = 16
M_SH_MIX, M_SC_MIX, M_G_MIX, M_SH_MLP, M_SC_MLP, M_G_MLP = range(N_MOD)


def _cparams(sem):
    return pltpu.CompilerParams(dimension_semantics=sem, vmem_limit_bytes=VMEM_LIMIT_BYTES)


def _unit(xf):
    ms = jnp.mean(xf * xf, axis=-1, keepdims=True)
    return xf * lax.rsqrt(ms + EPS)


def _rms(xf, g):
    return _unit(xf) * g


def _silu(v):
    return v * jax.nn.sigmoid(v)


def _const_spec(block, index):
    return pl.BlockSpec(block, lambda *_: index, pipeline_mode=pl.Buffered(1))


def _vec_spec(vecs, layer):
    return _const_spec((1,) + vecs.shape[1:], (layer, 0, 0))


def _mod_spec(mods, layer, chunk):
    d = mods.shape[-1] // N_MOD
    return _const_spec((1, mods.shape[1], d), (layer, 0, chunk))


def _vec(v_ref, row, width=None):
    return v_ref[0, row:row + 1, :] if width is None else v_ref[0, row:row + 1, 0:width]


def _cond(m_ref, cond_row):
    if cond_row is None:
        return m_ref[0, pl.ds(pl.program_id(0), 1), :]
    return m_ref[0, cond_row:cond_row + 1, :]


def _mod_kernel(c_ref, w_ref, b_ref, o_ref):
    s = _silu(c_ref[...]).astype(BF16)
    o_ref[0] = jnp.dot(s, w_ref[0].astype(BF16), preferred_element_type=F32) + b_ref[0]


def _modulation(cond, w_mod, b_mod):
    depth, d, n = w_mod.shape
    tn = 1024
    return pl.pallas_call(
        _mod_kernel,
        grid=(depth, n // tn),
        in_specs=[pl.BlockSpec((8, d), lambda l, j: (0, 0)),
                  pl.BlockSpec((1, d, tn), lambda l, j: (l, 0, j)),
                  pl.BlockSpec((1, 1, tn), lambda l, j: (l, 0, j))],
        out_specs=pl.BlockSpec((1, 8, tn), lambda l, j: (l, 0, j)),
        out_shape=jax.ShapeDtypeStruct((depth, 8, n), F32),
        compiler_params=_cparams(("arbitrary", "arbitrary")),
        name="modulation",
    )(cond, w_mod, b_mod.reshape(depth, 1, n))


def _rope(v, cos, sin):
    lane = lax.broadcasted_iota(jnp.int32, v.shape, 1)
    partner = jnp.where((lane % 64) < 32, pltpu.roll(v, 96, 1), pltpu.roll(v, 32, 1))
    return v * cos + partner * sin


def _inproj_kernel(*refs, rope, cond_row, n_cast):
    n_in = 7 if rope else 5
    if rope:
        x_ref, v_ref, sc_ref, sh_ref, w_ref, cos_ref, sin_ref = refs[:n_in]
    else:
        x_ref, v_ref, sc_ref, sh_ref, w_ref = refs[:n_in]
    outs = refs[n_in + n_cast:]
    xbc_ref, z_ref, q_ref, k_ref, vt_ref, dt_ref = outs[:6]
    h_scr = outs[-1]
    for src, dst in zip(refs[n_in:n_in + n_cast], outs[6:6 + n_cast]):
        dst[...] = src[...].astype(BF16)
    h = (_rms(x_ref[0], _vec(v_ref, V_G_PRE_MIX)) * (1.0 + _cond(sc_ref, cond_row))
         + _cond(sh_ref, cond_row))
    h_scr[...] = h.astype(BF16)

    nt = (((1,), (1,)), ((), ()))

    def proj(r0, n):
        return lax.dot_general(h_scr[...], w_ref[0, r0:r0 + n, :], nt, preferred_element_type=F32)

    seg = 512
    for s in range(D_XBC // seg):
        xbc_ref[0, :, s * seg:(s + 1) * seg] = proj(s * seg, seg).astype(BF16)
    for s in range(D_SSM // seg):
        z_ref[0, :, s * seg:(s + 1) * seg] = proj(D_XBC + s * seg, seg).astype(BF16)

    dt0 = D_XBC + D_SSM
    lane = lax.broadcasted_iota(jnp.int32, (x_ref.shape[1], LANES), 1)
    dt_ref[0] = jnp.where(lane < N_DT, proj(dt0, LANES), 0.0)

    scale = ATTN_HEAD_DIM ** -0.5 * LOG2_E
    hd = ATTN_HEAD_DIM
    q0 = dt0 + N_DT
    for s in range(D_ATTN // seg):
        acc = proj(q0 + s * seg, seg)
        for hh in range(seg // hd):
            v = acc[:, hh * hd:(hh + 1) * hd]
            if rope:
                v = _rope(v, cos_ref[...], sin_ref[...])
            q_ref[0, :, s * seg + hh * hd:s * seg + (hh + 1) * hd] = (v * scale).astype(BF16)
    acc = proj(q0 + D_ATTN, KV_DIM)
    for hh in range(ATTN_KV_HEADS):
        v = acc[:, hh * hd:(hh + 1) * hd]
        if rope:
            v = _rope(v, cos_ref[...], sin_ref[...])
        k_ref[0, :, hh * hd:(hh + 1) * hd] = v.astype(BF16)
    v0 = q0 + D_ATTN + KV_DIM
    vt_ref[0] = lax.dot_general(w_ref[0, v0:v0 + KV_DIM, :], h_scr[...], nt,
                                preferred_element_type=F32).astype(BF16)


def _cast_slabs(weights, layer, nsteps, steps_per_batch):
    specs, out_specs, out_shape = [], [], []
    for w in weights:
        rows = w.shape[1] // nsteps
        if w.shape[1] % nsteps or rows % BF16_SUBLANES or rows * w.shape[2] * 4 > CAST_SLAB_BYTES:
            return None
        blk = (1, rows, w.shape[2])
        specs.append(pl.BlockSpec(blk, lambda bi, i: (layer, bi * steps_per_batch + i, 0)))
        out_specs.append(pl.BlockSpec(blk, lambda bi, i: (0, bi * steps_per_batch + i, 0)))
        out_shape.append(jax.ShapeDtypeStruct((1,) + w.shape[1:], BF16))
    return specs, out_specs, out_shape


def _inproj(x, vecs, mods, w_in_t, tables, *, layer, cond_row, tm, cast=()):
    b, l, d = x.shape
    rope = tables is not None
    row = lambda w: pl.BlockSpec((1, tm, w), lambda bi, i: (bi, i, 0))
    tab = pl.BlockSpec((tm, LANES), lambda bi, i: (i, 0))
    widths = [(D_XBC, BF16), (D_SSM, BF16), (D_ATTN, BF16), (KV_DIM, BF16)]
    out_specs = [row(w) for w, _ in widths] + [pl.BlockSpec((1, KV_DIM, tm), lambda bi, i: (bi, 0, i)), row(LANES)]
    out_shape = ([jax.ShapeDtypeStruct((b, l, w), dt) for w, dt in widths]
                 + [jax.ShapeDtypeStruct((b, KV_DIM, l), BF16), jax.ShapeDtypeStruct((b, l, LANES), F32)])
    in_specs = [row(d), _vec_spec(vecs, layer), _mod_spec(mods, layer, M_SC_MIX), _mod_spec(mods, layer, M_SH_MIX),
                _const_spec((1,) + w_in_t.shape[1:], (layer, 0, 0))]
    args = [x, vecs, mods, mods, w_in_t]
    if rope:
        in_specs += [tab, tab]
        args += list(tables)
    if cast:
        c_in, c_out, c_shape = _cast_slabs(cast, layer, b * (l // tm), l // tm)
        in_specs += c_in
        args += list(cast)
        out_specs += c_out
        out_shape += c_shape
    return pl.pallas_call(
        functools.partial(_inproj_kernel, rope=rope, cond_row=cond_row, n_cast=len(cast)),
        grid=(b, l // tm),
        in_specs=in_specs,
        out_specs=out_specs,
        out_shape=out_shape,
        scratch_shapes=[pltpu.VMEM((tm, d), BF16)],
        compiler_params=_cparams(("arbitrary", "arbitrary")),
        name="inproj_rope" if rope else "inproj",
    )(*args)


def _conv_kernel(u_ref, prev_ref, next_ref, w_ref, v_ref, o_ref, ext_scr, *, t):
    i = pl.program_id(1)
    n = pl.num_programs(1)
    halo = SSM_CONV // 2
    h16 = BF16_SUBLANES
    zero = jnp.zeros(prev_ref.shape[1:], BF16)
    ext_scr[0:h16, :] = jnp.where(i > 0, prev_ref[0], zero)
    ext_scr[h16:h16 + t, :] = u_ref[0]
    ext_scr[h16 + t:h16 + t + h16, :] = jnp.where(i < n - 1, next_ref[0], zero)
    blk = CONV_ROWS
    win = blk + 2 * h16
    rr = lax.broadcasted_iota(jnp.int32, (blk, SSM_CONV * win), 0)
    cc = lax.broadcasted_iota(jnp.int32, (blk, SSM_CONV * win), 1)
    hit = cc == rr + h16 - halo
    for k in range(1, SSM_CONV):
        hit = hit | (cc == rr + k * win + h16 + k - halo)
    shift_all = hit.astype(F32).astype(BF16)
    taps = w_ref[0].astype(BF16)
    bias = _vec(v_ref, V_CONV_B, D_XBC)
    for r in range(t // blk):
        window = ext_scr[r * blk:r * blk + win, :]
        scaled = jnp.concatenate([window * taps[k:k + 1, :] for k in range(SSM_CONV)], axis=0)
        acc = jnp.dot(shift_all, scaled, preferred_element_type=F32) + bias
        o_ref[0, r * blk:(r + 1) * blk, :] = _silu(acc).astype(BF16)


def _conv_silu(u, conv_w, vecs, *, layer, t):
    b, l, c = u.shape
    h16 = BF16_SUBLANES
    nh = l // h16
    per = t // h16
    return pl.pallas_call(
        functools.partial(_conv_kernel, t=t),
        grid=(b, l // t),
        in_specs=[pl.BlockSpec((1, t, c), lambda bi, i: (bi, i, 0)),
                  pl.BlockSpec((1, h16, c), lambda bi, i: (bi, jnp.maximum(i * per - 1, 0), 0)),
                  pl.BlockSpec((1, h16, c), lambda bi, i: (bi, jnp.minimum((i + 1) * per, nh - 1), 0)),
                  _const_spec((1,) + conv_w.shape[1:], (layer, 0, 0)), _vec_spec(vecs, layer)],
        out_specs=pl.BlockSpec((1, t, c), lambda bi, i: (bi, i, 0)),
        out_shape=jax.ShapeDtypeStruct((b, l, c), BF16),
        scratch_shapes=[pltpu.VMEM((t + 2 * h16, c), BF16)],
        compiler_params=_cparams(("arbitrary", "arbitrary")),
        name="conv_silu",
    )(u, u, u, conv_w, vecs)


def _split3(v):
    hi = v.astype(BF16)
    r1 = v - hi.astype(F32)
    mid = r1.astype(BF16)
    lo = (r1 - mid.astype(F32)).astype(BF16)
    return hi, mid, lo


def _ssd_kernel(*refs):
    for stage in _ssd_stages(*refs):
        stage()


def _ssd_stages(xf_ref, xb_ref, dtf_ref, dtb_ref, v_ref, h0f_ref, h0b_ref,
                yf_ref, yb_ref, hff_ref, hfb_ref, ef_scr, eb_scr):
    @pl.when(pl.program_id(1) == 0)
    def _():
        r = lax.broadcasted_iota(jnp.int32, ef_scr.shape, 0) % LANES
        col = lax.broadcasted_iota(jnp.int32, ef_scr.shape, 1)
        for h0_ref, hf_ref, e_scr, off in ((h0f_ref, hff_ref, ef_scr, 0), (h0b_ref, hfb_ref, eb_scr, SSM_HEADS)):
            hf_ref[0] = h0_ref[0]
            e_scr[...] = (r == col // SSM_HEAD_DIM + off).astype(BF16)

    bias = _vec(v_ref, V_DT_BIAS, LANES)
    a_log = _vec(v_ref, V_A_LOG, LANES)
    q = SSM_CHUNK
    n = xf_ref.shape[1] // q
    order = [(j, n - 1 - j) for j in range(n)]
    preps = [(_ssd_prep(dtf_ref[0, jf * q:(jf + 1) * q, :], bias, a_log, ef_scr, rev=False),
              _ssd_prep(dtb_ref[0, jb * q:(jb + 1) * q, :], bias, a_log, eb_scr, rev=True)) for jf, jb in order]
    stages = []
    for (jf, jb), (prep_f, prep_b) in zip(order, preps):
        stages.append(functools.partial(_ssd_chunk, xf_ref, yf_ref, hff_ref, prep_f, slice(jf * q, (jf + 1) * q),
                                        rev=False))
        stages.append(functools.partial(_ssd_chunk, xb_ref, yb_ref, hfb_ref, prep_b, slice(jb * q, (jb + 1) * q),
                                        rev=True))
    return stages


def _ssd_prep(dt_raw, bias, a_log, e_scr, *, rev):
    q = SSM_CHUNK
    x = dt_raw + bias
    dt = jnp.maximum(x, 0.0) + jnp.log1p(jnp.exp(-jnp.abs(x)))
    a2 = -jnp.exp(a_log) * LOG2_E
    dta = dt * a2

    ri = lax.broadcasted_iota(jnp.int32, (q, q), 0)
    ci = lax.broadcasted_iota(jnp.int32, (q, q), 1)
    mask = (ci >= ri) if rev else (ci <= ri)
    tri = mask.astype(BF16)
    cs = sum(jnp.dot(tri, part, preferred_element_type=F32) for part in _split3(dta))
    tot = cs[0:1] if rev else cs[q - 1:q]
    row_t = (cs - jnp.log2(dt)).T

    ecs = jnp.exp2(cs)
    ecs_hi = ecs.astype(BF16)
    ecs_lo = (ecs - ecs_hi.astype(F32)).astype(BF16)
    w_end = (dt * jnp.exp2(tot - cs)).astype(BF16)
    wx = jnp.dot(w_end, e_scr[0:LANES, :], preferred_element_type=F32)
    ecsx = jnp.dot(jnp.concatenate([ecs_hi, ecs_lo], axis=1), e_scr[...], preferred_element_type=F32)
    return dict(mask=mask, cs=cs, row_t=row_t, wx=wx, ecsx=ecsx)


def _ssd_chunk(xbc_ref, y_ref, hf_ref, prep, rows, *, rev):
    q = SSM_CHUNK
    off = SSM_HEADS if rev else 0
    mask, cs, row_t, wx, ecsx = prep["mask"], prep["cs"], prep["row_t"], prep["wx"], prep["ecsx"]
    cdec = ecsx[0:1] if rev else ecsx[q - 1:q]

    lane_lo = lax.broadcasted_iota(jnp.int32, (q, LANES), 1) < SSM_HEAD_DIM
    nt = (((1,), (1,)), ((), ()))
    for g in range(SSM_GROUPS):
        gc = slice(g * GROUP_COLS, (g + 1) * GROUP_COLS)
        bg = xbc_ref[0, rows, D_SSM + g * SSM_STATE:D_SSM + (g + 1) * SSM_STATE]
        cg = xbc_ref[0, rows, D_SSM + SSM_BC + g * SSM_STATE:D_SSM + SSM_BC + (g + 1) * SSM_STATE]
        cb = lax.dot_general(cg, bg, nt, preferred_element_type=F32)
        s_in = hf_ref[0, :, gc]
        y_off = jnp.dot(cg, s_in.astype(BF16), preferred_element_type=F32) * ecsx[:, gc]
        xg = xbc_ref[0, rows, gc]
        xw = (xg.astype(F32) * wx[:, gc]).astype(BF16)
        bg_t = bg.astype(F32).T.astype(BF16)
        hf_ref[0, :, gc] = s_in * cdec[:, gc] + jnp.dot(bg_t, xw, preferred_element_type=F32)

        for jp in range(HEADS_PER_GROUP // 2):
            pair = g * (HEADS_PER_GROUP // 2) + jp

            def head_mat(h):
                seg = cs[:, off + h:off + h + 1] - row_t[off + h:off + h + 1, :]
                return (cb * jnp.exp2(jnp.where(mask, seg, -jnp.inf))).astype(BF16)

            lhs = jnp.concatenate([head_mat(2 * pair), head_mat(2 * pair + 1)], axis=1)
            xp = xbc_ref[0, rows, pair * LANES:(pair + 1) * LANES]
            zero = jnp.zeros_like(xp)
            rhs = jnp.concatenate([jnp.where(lane_lo, xp, zero), jnp.where(lane_lo, zero, xp)], axis=0)
            y_d = jnp.dot(lhs, rhs, preferred_element_type=F32)
            y_ref[0, rows, pair * LANES:(pair + 1) * LANES] = (
                y_d + y_off[:, jp * LANES:(jp + 1) * LANES]).astype(BF16)


def _call(kernel_fn, parts, name):
    grid = parts[0]["grid"]
    assert all(p["grid"] == grid for p in parts)
    cat = lambda key: [v for p in parts for v in p[key]]
    return pl.pallas_call(
        kernel_fn,
        grid=grid,
        in_specs=cat("in_specs"),
        out_specs=cat("out_specs"),
        out_shape=cat("out_shape"),
        scratch_shapes=cat("scratch"),
        compiler_params=_cparams(("arbitrary",) * len(grid)),
        name=name,
    )(*cat("args"))


def _ssd_parts(xbc, dt_raw, vecs, h0_f, h0_b, *, layer):
    b, l, _ = xbc.shape
    q = min(l, SSM_CHUNK * SSD_CHUNKS_PER_STEP)
    nc = l // q
    fwd = lambda w: pl.BlockSpec((1, q, w), lambda bi, c: (bi, c, 0))
    bwd = lambda w: pl.BlockSpec((1, q, w), lambda bi, c: (bi, nc - 1 - c, 0))
    st = pl.BlockSpec((1, SSM_STATE, D_SSM), lambda bi, c: (bi, 0, 0))
    y_shape = jax.ShapeDtypeStruct((b, l, D_SSM), BF16)
    h_shape = jax.ShapeDtypeStruct((b, SSM_STATE, D_SSM), F32)
    return dict(
        grid=(b, nc),
        in_specs=[fwd(D_XBC), bwd(D_XBC), fwd(LANES), bwd(LANES), _vec_spec(vecs, layer), st, st],
        args=[xbc, xbc, dt_raw, dt_raw, vecs, h0_f, h0_b],
        out_specs=[fwd(D_SSM), bwd(D_SSM), st, st],
        out_shape=[y_shape, y_shape, h_shape, h_shape],
        scratch=[pltpu.VMEM((2 * LANES, D_SSM), BF16), pltpu.VMEM((2 * LANES, D_SSM), BF16)])


def _ssd(xbc, dt_raw, vecs, h0_f, h0_b, *, layer):
    return _call(_ssd_kernel, [_ssd_parts(xbc, dt_raw, vecs, h0_f, h0_b, layer=layer)], "ssd")


def _scan_attn_kernel(*refs, n_ssd_in, n_attn_in):
    ssd_in, attn_in = refs[:n_ssd_in], refs[n_ssd_in:n_ssd_in + n_attn_in]
    outs = refs[n_ssd_in + n_attn_in:]
    ssd = _ssd_stages(*ssd_in, *outs[:4], *outs[5:])
    attn = _attn_stages(*attn_in, outs[4], band=True)
    for n in range(max(len(ssd), len(attn))):
        for stages in (attn, ssd):
            if n < len(stages):
                stages[n]()


def _ssd_attention(xbc, dt_raw, vecs, h0_f, h0_b, q, k, vt, kx, vtx, *, layer):
    ssd = _ssd_parts(xbc, dt_raw, vecs, h0_f, h0_b, layer=layer)
    att = _attn_parts(q, k, vt, kx, vtx, vecs, layer=layer, band=True)
    fn = functools.partial(_scan_attn_kernel, n_ssd_in=len(ssd["args"]), n_attn_in=len(att["args"]))
    return _call(fn, [ssd, att], "ssd_attn")


def _attn_stages(*refs, band):
    if band:
        q_ref, kp_ref, kc_ref, kn_ref, vp_ref, vc_ref, vn_ref, kx_ref, vx_ref, v_ref, o_ref = refs
    else:
        q_ref, kx_ref, vx_ref, v_ref, o_ref = refs
    i = pl.program_id(1)
    t = ATTN_BLOCK
    hd = ATTN_HEAD_DIM
    rep = ATTN_HEADS // ATTN_KV_HEADS
    tq = q_ref.shape[1]
    nsub = tq // t if band else 1
    rows_per = t if band else tq
    nblk = pl.num_programs(1) * nsub
    lc = kx_ref.shape[1]
    nk = (3 * t if band else 0) + lc
    nq = rep * rows_per
    nt = (((1,), (1,)), ((), ()))

    if band:
        krow = lax.broadcasted_iota(jnp.int32, (t, nq), 0)
        qcol = lax.broadcasted_iota(jnp.int32, (t, nq), 1) % t
    qhead = lax.broadcasted_iota(jnp.int32, (1, nq), 1) // rows_per
    ones = jnp.ones((BF16_SUBLANES, nk), BF16)
    sinks = _vec(v_ref, V_SINK, LANES) * LOG2_E

    def scores(g, j):
        ks = slice(g * hd, (g + 1) * hd)
        rows = slice(j * rows_per, (j + 1) * rows_per)
        qg = jnp.concatenate([q_ref[0, rows, (g * rep + r) * hd:(g * rep + r + 1) * hd] for r in range(rep)],
                             axis=0)
        if band:
            band_k = jnp.concatenate([kp_ref[0, :, ks], kc_ref[0, :, ks], kn_ref[0, :, ks]], axis=0)
            k_all = jnp.concatenate([band_k[j * t:(j + 3) * t], kx_ref[0, :, ks]], axis=0)
        else:
            k_all = kx_ref[0, :, ks]
        s = lax.dot_general(k_all, qg, nt, preferred_element_type=F32)
        if band:
            qb = i * nsub + j
            lo = qcol + jnp.where(qb > 0, 0, 2 * t)
            hi = qcol - jnp.where(qb < nblk - 1, 0, 2 * t)
            s = jnp.concatenate([jnp.where(krow >= lo, s[0:t], -jnp.inf), s[t:2 * t],
                                 jnp.where(krow <= hi, s[2 * t:3 * t], -jnp.inf), s[3 * t:]], axis=0)
        return s

    def finish(g, j, s):
        ks = slice(g * hd, (g + 1) * hd)
        rows = slice(j * rows_per, (j + 1) * rows_per)
        sink = jnp.zeros((1, nq), F32)
        for r in range(rep):
            sink = jnp.where(qhead == r, sinks[:, g * rep + r:g * rep + r + 1], sink)
        if band:
            band_v = jnp.concatenate([vp_ref[0, ks, :], vc_ref[0, ks, :], vn_ref[0, ks, :]], axis=1)
            v_all = jnp.concatenate([band_v[:, j * t:(j + 3) * t], vx_ref[0, ks, :]], axis=1)
        else:
            v_all = vx_ref[0, ks, :]
        m = jnp.maximum(jnp.max(s, axis=0, keepdims=True), sink)
        p = jnp.exp2(s - m).astype(BF16)
        ov = jnp.dot(jnp.concatenate([v_all, ones], axis=0), p, preferred_element_type=F32)
        inv = 1.0 / (ov[hd:hd + 1] + jnp.exp2(sink - m))
        o = ov[0:hd] * inv
        for r in range(rep):
            o_ref[0, rows, (g * rep + r) * hd:(g * rep + r + 1) * hd] = (
                o[:, r * rows_per:(r + 1) * rows_per].T.astype(BF16))

    chains = [(g, j) for g in range(ATTN_KV_HEADS) for j in range(nsub)]
    pending = [scores(*chains[0])]

    def stage(n):
        s_cur = pending.pop()
        if n + 1 < len(chains):
            pending.append(scores(*chains[n + 1]))
        finish(*chains[n], s_cur)

    return [functools.partial(stage, n) for n in range(len(chains))]


def _attn_kernel(*refs, band):
    for stage in _attn_stages(*refs, band=band):
        stage()


def _attn_parts(q, k, vt, kx, vtx, vecs, *, layer, band):
    b, l, _ = q.shape
    t = ATTN_BLOCK
    tq = min(l, 4 * t)
    nsub = tq // t
    nblk = l // t
    lc = kx.shape[1] // b
    qs = pl.BlockSpec((1, tq, D_ATTN), lambda bi, i: (bi, i, 0))
    kctx = pl.BlockSpec((1, lc, KV_DIM), lambda bi, i: (0, bi, 0))
    vctx = pl.BlockSpec((1, KV_DIM, lc), lambda bi, i: (0, 0, bi))
    vs = _vec_spec(vecs, layer)
    if band:
        prv = lambda i: jnp.maximum(i * nsub - 1, 0)
        nxt = lambda i: jnp.minimum((i + 1) * nsub, nblk - 1)
        kspec = lambda rows, f: pl.BlockSpec((1, rows, KV_DIM), lambda bi, i: (bi, f(i), 0))
        vspec = lambda rows, f: pl.BlockSpec((1, KV_DIM, rows), lambda bi, i: (bi, 0, f(i)))
        cur = lambda i: i
        in_specs = [qs, kspec(t, prv), kspec(tq, cur), kspec(t, nxt),
                    vspec(t, prv), vspec(tq, cur), vspec(t, nxt), kctx, vctx, vs]
        args = [q, k, k, k, vt, vt, vt, kx, vtx, vecs]
    else:
        in_specs = [qs, kctx, vctx, vs]
        args = [q, kx, vtx, vecs]
    return dict(grid=(b, l // tq), in_specs=in_specs, args=args, out_specs=[qs],
                out_shape=[jax.ShapeDtypeStruct((b, l, D_ATTN), BF16)], scratch=[])


def _attention(q, k, vt, kx, vtx, vecs, *, layer, band):
    parts = _attn_parts(q, k, vt, kx, vtx, vecs, layer=layer, band=band)
    return _call(functools.partial(_attn_kernel, band=band), [parts], "attn_band" if band else "attn_ctx")[0]


def _outproj_kernel(yf_ref, yb_ref, xs_ref, z_ref, ya_ref, x_ref, w_ref, v_ref, gate_ref, scf_ref, shf_ref,
                    xo_ref, hm_ref, *, sub, cond_row):
    d_skip = _vec(v_ref, V_DSKIP, D_SSM)
    g_ssm = _vec(v_ref, V_SSM_NORM, D_SSM)
    gate = _cond(gate_ref, cond_row) * _vec(v_ref, V_G_POST_MIX)
    g_pre = _vec(v_ref, V_G_PRE_MLP) * (1.0 + _cond(scf_ref, cond_row))
    shift = _cond(shf_ref, cond_row)
    for r in range(x_ref.shape[1] // sub):
        rows = slice(r * sub, (r + 1) * sub)
        y = (yf_ref[0, rows, :].astype(F32) + yb_ref[0, rows, :].astype(F32)
             + xs_ref[0, rows, :].astype(F32) * d_skip)
        y = y * _silu(z_ref[0, rows, :].astype(F32))
        cat = jnp.concatenate([_rms(y, g_ssm).astype(BF16), ya_ref[0, rows, :]], axis=1)
        mix = jnp.dot(cat, w_ref[0], preferred_element_type=F32)
        xn = x_ref[0, rows, :] + _unit(mix) * gate
        xo_ref[0, rows, :] = xn
        hm_ref[0, rows, :] = (_unit(xn) * g_pre + shift).astype(BF16)


def _outproj(yf, yb, xbc, z, ya, x, w_out_b, vecs, mods, *, layer, w_layer, cond_row, tm):
    b, l, d = x.shape
    row = lambda w: pl.BlockSpec((1, tm, w), lambda bi, i: (bi, i, 0))
    return pl.pallas_call(
        functools.partial(_outproj_kernel, sub=min(tm, 256), cond_row=cond_row),
        grid=(b, l // tm),
        in_specs=[row(D_SSM), row(D_SSM), row(D_SSM), row(D_SSM), row(D_ATTN), row(d),
                  _const_spec((1,) + w_out_b.shape[1:], (w_layer, 0, 0)), _vec_spec(vecs, layer),
                  _mod_spec(mods, layer, M_G_MIX), _mod_spec(mods, layer, M_SC_MLP),
                  _mod_spec(mods, layer, M_SH_MLP)],
        out_specs=[row(d), row(d)],
        out_shape=[jax.ShapeDtypeStruct((b, l, d), F32), jax.ShapeDtypeStruct((b, l, d), BF16)],
        compiler_params=_cparams(("arbitrary", "arbitrary")),
        name="outproj",
    )(yf, yb, xbc, z, ya, x, w_out_b, vecs, mods, mods, mods)


def _mlp_kernel(h_ref, x_ref, w1_ref, w2_ref, v_ref, gate_ref, xo_ref, acc_ref, *, cond_row):
    k = pl.program_id(2)

    @pl.when(k == 0)
    def _():
        acc_ref[...] = jnp.zeros_like(acc_ref)

    a = jnp.dot(h_ref[0], w1_ref[0], preferred_element_type=F32)
    a = jnp.square(jnp.maximum(a, 0.0)).astype(BF16)
    acc_ref[...] += jnp.dot(a, w2_ref[0], preferred_element_type=F32)

    @pl.when(k == pl.num_programs(2) - 1)
    def _():
        xo_ref[0] = x_ref[0] + _cond(gate_ref, cond_row) * _rms(acc_ref[...], _vec(v_ref, V_G_POST_MLP))


def _mlp(h, x, w1_b, w2_b, vecs, mods, *, layer, w_layer, cond_row, tm, tf):
    b, l, d = x.shape
    dff = w1_b.shape[2]
    row = pl.BlockSpec((1, tm, d), lambda bi, i, k: (bi, i, 0))
    return pl.pallas_call(
        functools.partial(_mlp_kernel, cond_row=cond_row),
        grid=(b, l // tm, dff // tf),
        in_specs=[row, row,
                  pl.BlockSpec((1, d, tf), lambda bi, i, k: (w_layer, 0, k)),
                  pl.BlockSpec((1, tf, d), lambda bi, i, k: (w_layer, k, 0)),
                  _vec_spec(vecs, layer), _mod_spec(mods, layer, M_G_MLP)],
        out_specs=row,
        out_shape=jax.ShapeDtypeStruct((b, l, d), F32),
        scratch_shapes=[pltpu.VMEM((tm, d), F32)],
        compiler_params=_cparams(("arbitrary", "arbitrary", "arbitrary")),
        name="mlp",
    )(h, x, w1_b, w2_b, vecs, mods)


def _rope_tables(l):
    n_freq = ATTN_HEAD_DIM // 4
    inv_freq = ROPE_BASE ** (-np.arange(n_freq, dtype=np.float64) / n_freq)
    t = np.arange(l)
    ang_r = (t // GRID_W)[:, None] * inv_freq[None, :]
    ang_c = (t % GRID_W)[:, None] * inv_freq[None, :]
    cos = np.concatenate([np.cos(ang_r)] * 2 + [np.cos(ang_c)] * 2, axis=-1)
    sin = np.concatenate([-np.sin(ang_r), np.sin(ang_r), -np.sin(ang_c), np.sin(ang_c)], axis=-1)
    return jnp.asarray(cos, F32), jnp.asarray(sin, F32)


def _pack_vectors(d, g_pre_mix, g_post_mix, g_pre_mlp, g_post_mlp, ssm_norm, d_skip, conv_b, dt_bias, a_log,
                  attn_sink):
    depth = g_pre_mix.shape[0]

    def r(v):
        v = v.reshape(depth, 1, -1).astype(F32)
        return jnp.pad(v, ((0, 0), (0, 0), (0, d - v.shape[-1])))

    rows = [None] * VEC_ROWS
    rows[V_G_PRE_MIX], rows[V_G_POST_MIX] = r(g_pre_mix), r(g_post_mix)
    rows[V_G_PRE_MLP], rows[V_G_POST_MLP] = r(g_pre_mlp), r(g_post_mlp)
    rows[V_SSM_NORM], rows[V_DSKIP] = r(ssm_norm), r(jnp.repeat(d_skip, SSM_HEAD_DIM, axis=-1))
    rows[V_CONV_B], rows[V_DT_BIAS], rows[V_A_LOG], rows[V_SINK] = r(conv_b), r(dt_bias), r(a_log), r(attn_sink)
    zero = jnp.zeros((depth, 1, d), F32)
    return jnp.concatenate([zero if v is None else v for v in rows], axis=1)


def kernel(x, c, ctx, c_ctx, w_mod, b_mod, g_pre_mix, g_post_mix, g_pre_mlp, g_post_mlp, w_in, conv_w,
           conv_b, a_log, dt_bias, d_skip, ssm_norm, attn_sink, w_out, w_ff1, w_ff2):
    b, l, d = x.shape
    lc = ctx.shape[1]
    depth = w_mod.shape[0]
    assert b <= 7 and l % ROW_TILE == 0 and lc % SSM_CHUNK == 0 and w_in.shape[2] == D_IN_PROJ

    cond = jnp.concatenate([c, c_ctx[None, :], jnp.zeros((8 - b - 1, d), F32)], axis=0)
    ctx_row = b
    mods = _modulation(cond, w_mod, b_mod)
    vecs = _pack_vectors(d, g_pre_mix, g_post_mix, g_pre_mlp, g_post_mlp, ssm_norm, d_skip, conv_b, dt_bias,
                         a_log, attn_sink)
    w_in_t = jnp.swapaxes(w_in, 1, 2).astype(BF16)
    later_weights = (w_ff1, w_ff2, w_out)
    steps = b * (l // ROW_TILE)
    side_cast = _cast_slabs(later_weights, 0, steps, l // ROW_TILE) is not None
    if not side_cast:
        w1_b, w2_b, wo_b = (w.astype(BF16) for w in later_weights)
    tables = _rope_tables(l)
    h_zero = jnp.zeros((b, SSM_STATE, D_SSM), F32)

    nc = b * lc
    tm_c = min(nc, ROW_TILE)
    flat = lambda a: a.reshape(1, nc, a.shape[-1])
    per_batch = lambda a: a.reshape(b, lc, a.shape[-1])
    xc = flat(ctx)
    for i in range(depth):
        last = i == depth - 1
        xbc_c, z_c, q_c, k_c, vt_c, dt_c = _inproj(xc, vecs, mods, w_in_t, None, layer=i, cond_row=ctx_row, tm=tm_c)
        xbc_c = _conv_silu(per_batch(xbc_c), conv_w, vecs, layer=i, t=min(lc, ROW_TILE))
        yf_c, yb_c, h_f, h_b = _ssd(xbc_c, per_batch(dt_c), vecs, h_zero, h_zero, layer=i)

        outs = _inproj(x, vecs, mods, w_in_t, tables, layer=i, cond_row=None, tm=ROW_TILE,
                       cast=later_weights if side_cast else ())
        xbc_l, z_l, q_l, k_l, vt_l, dt_l = outs[:6]
        w1_i, w2_i, wo_i, w_layer = (*outs[6:9], 0) if side_cast else (w1_b, w2_b, wo_b, i)
        xbc_l = _conv_silu(xbc_l, conv_w, vecs, layer=i, t=ROW_TILE)
        yf, yb, _, _, ya = _ssd_attention(xbc_l, dt_l, vecs, h_f, h_b, q_l, k_l, vt_l, k_c, vt_c, layer=i)
        x, hm = _outproj(yf, yb, xbc_l, z_l, ya, x, wo_i, vecs, mods, layer=i, w_layer=w_layer, cond_row=None,
                         tm=ROW_TILE)
        x = _mlp(hm, x, w1_i, w2_i, vecs, mods, layer=i, w_layer=w_layer, cond_row=None, tm=ROW_TILE, tf=FF_TILE)

        if not last:
            ya_c = _attention(per_batch(q_c), None, None, k_c, vt_c, vecs, layer=i, band=False)
            xc, hm_c = _outproj(flat(yf_c), flat(yb_c), flat(xbc_c), z_c, flat(ya_c), xc, wo_i, vecs, mods,
                                layer=i, w_layer=w_layer, cond_row=ctx_row, tm=tm_c)
            xc = _mlp(hm_c, xc, w1_i, w2_i, vecs, mods, layer=i, w_layer=w_layer, cond_row=ctx_row, tm=tm_c,
                      tf=FF_TILE)
    return x
```

```python
import functools

import numpy as np
import jax
import jax.numpy as jnp
from jax import lax
from jax.experimental import pallas as pl
from jax.experimental.pallas import tpu as pltpu

F32 = jnp.float32
BF16 = jnp.bfloat16

N_MOD = 6
SSM_HEAD_DIM = 64
SSM_HEADS = 16
SSM_STATE = 128
SSM_GROUPS = 2
SSM_CONV = 5
SSM_CHUNK = 128
ATTN_HEAD_DIM = 128
ATTN_HEADS = 8
ATTN_KV_HEADS = 2
ATTN_BLOCK = 128
GRID_W = 64
ROPE_BASE = 10000.0
EPS = 1e-6
LOG2_E = 1.4426950408889634

D_SSM = SSM_HEADS * SSM_HEAD_DIM
SSM_BC = SSM_GROUPS * SSM_STATE
D_XBC = D_SSM + 2 * SSM_BC
D_ATTN = ATTN_HEADS * ATTN_HEAD_DIM
KV_DIM = ATTN_KV_HEADS * ATTN_HEAD_DIM
N_DT = 2 * SSM_HEADS
D_IN_PROJ = D_XBC + D_SSM + N_DT + D_ATTN + 2 * KV_DIM
HEADS_PER_GROUP = SSM_HEADS // SSM_GROUPS
GROUP_COLS = HEADS_PER_GROUP * SSM_HEAD_DIM

LANES = 128
F32_SUBLANES = 8
BF16_SUBLANES = 16
VMEM_LIMIT_BYTES = 56 * 1024 * 1024

ROW_TILE = 512
FF_TILE = 1024
MOD_TILE = 1024
PROJ_SEG = 512
OUT_SUB_ROWS = 256
COND_ROWS = F32_SUBLANES
CONV_ROWS = 128
SSD_CHUNKS_PER_STEP = 4
ATTN_BLOCKS_PER_STEP = 4
CAST_SLAB_BYTES = 2 * 1024 * 1024

V_G_PRE_MIX, V_G_POST_MIX, V_G_PRE_MLP, V_G_POST_MLP = 0, 1, 2, 3
V_SSM_NORM, V_DSKIP, V_CONV_B, V_DT_BIAS, V_A_LOG, V_SINK = 4, 5, 6, 7, 8, 9
VEC_ROWS = 16
M_SH_MIX, M_SC_MIX, M_G_MIX, M_SH_MLP, M_SC_MLP, M_G_MLP = range(N_MOD)


def _cparams(sem):
    return pltpu.CompilerParams(dimension_semantics=sem, vmem_limit_bytes=VMEM_LIMIT_BYTES)


def _unit(xf):
    ms = jnp.mean(xf * xf, axis=-1, keepdims=True)
    return xf * lax.rsqrt(ms + EPS)


def _rms(xf, g):
    return _unit(xf) * g


def _silu(v):
    return v * jax.nn.sigmoid(v)


def _const_spec(block, index):
    return pl.BlockSpec(block, lambda *_: index, pipeline_mode=pl.Buffered(1))


def _vec_spec(vecs, layer):
    return _const_spec((1,) + vecs.shape[1:], (layer, 0, 0))


def _mod_spec(mods, layer, chunk):
    d = mods.shape[-1] // N_MOD
    return _const_spec((1, mods.shape[1], d), (layer, 0, chunk))


def _vec(v_ref, row, width=None):
    return v_ref[0, row:row + 1, :] if width is None else v_ref[0, row:row + 1, 0:width]


def _cond(m_ref, cond_row):
    if cond_row is None:
        return m_ref[0, pl.ds(pl.program_id(0), 1), :]
    return m_ref[0, cond_row:cond_row + 1, :]


def _mod_kernel(c_ref, w_ref, b_ref, o_ref):
    s = _silu(c_ref[...]).astype(BF16)
    o_ref[0] = jnp.dot(s, w_ref[0].astype(BF16), preferred_element_type=F32) + b_ref[0]


def _modulation(cond, w_mod, b_mod):
    depth, d, n = w_mod.shape
    rows = cond.shape[0]
    tn = MOD_TILE
    return pl.pallas_call(
        _mod_kernel,
        grid=(depth, n // tn),
        in_specs=[pl.BlockSpec((rows, d), lambda l, j: (0, 0)),
                  pl.BlockSpec((1, d, tn), lambda l, j: (l, 0, j)),
                  pl.BlockSpec((1, 1, tn), lambda l, j: (l, 0, j))],
        out_specs=pl.BlockSpec((1, rows, tn), lambda l, j: (l, 0, j)),
        out_shape=jax.ShapeDtypeStruct((depth, rows, n), F32),
        compiler_params=_cparams(("arbitrary", "arbitrary")),
        name="modulation",
    )(cond, w_mod, b_mod.reshape(depth, 1, n))


def _rope(v, cos, sin):
    lane = lax.broadcasted_iota(jnp.int32, v.shape, 1)
    partner = jnp.where((lane % 64) < 32, pltpu.roll(v, 96, 1), pltpu.roll(v, 32, 1))
    return v * cos + partner * sin


def _inproj_kernel(*refs, rope, cond_row, n_cast):
    n_in = 7 if rope else 5
    if rope:
        x_ref, v_ref, sc_ref, sh_ref, w_ref, cos_ref, sin_ref = refs[:n_in]
    else:
        x_ref, v_ref, sc_ref, sh_ref, w_ref = refs[:n_in]
    outs = refs[n_in + n_cast:]
    xbc_ref, z_ref, q_ref, k_ref, vt_ref, dt_ref = outs[:6]
    h_scr = outs[-1]
    for src, dst in zip(refs[n_in:n_in + n_cast], outs[6:6 + n_cast]):
        dst[...] = src[...].astype(BF16)
    h = (_rms(x_ref[0], _vec(v_ref, V_G_PRE_MIX)) * (1.0 + _cond(sc_ref, cond_row))
         + _cond(sh_ref, cond_row))
    h_scr[...] = h.astype(BF16)

    nt = (((1,), (1,)), ((), ()))

    def proj(r0, n):
        return lax.dot_general(h_scr[...], w_ref[0, r0:r0 + n, :], nt, preferred_element_type=F32)

    seg = PROJ_SEG
    for s in range(D_XBC // seg):
        xbc_ref[0, :, s * seg:(s + 1) * seg] = proj(s * seg, seg).astype(BF16)
    for s in range(D_SSM // seg):
        z_ref[0, :, s * seg:(s + 1) * seg] = proj(D_XBC + s * seg, seg).astype(BF16)

    dt0 = D_XBC + D_SSM
    lane = lax.broadcasted_iota(jnp.int32, (x_ref.shape[1], LANES), 1)
    dt_ref[0] = jnp.where(lane < N_DT, proj(dt0, LANES), 0.0)

    scale = ATTN_HEAD_DIM ** -0.5 * LOG2_E
    hd = ATTN_HEAD_DIM
    q0 = dt0 + N_DT
    for s in range(D_ATTN // seg):
        acc = proj(q0 + s * seg, seg)
        for hh in range(seg // hd):
            v = acc[:, hh * hd:(hh + 1) * hd]
            if rope:
                v = _rope(v, cos_ref[...], sin_ref[...])
            q_ref[0, :, s * seg + hh * hd:s * seg + (hh + 1) * hd] = (v * scale).astype(BF16)
    acc = proj(q0 + D_ATTN, KV_DIM)
    for hh in range(ATTN_KV_HEADS):
        v = acc[:, hh * hd:(hh + 1) * hd]
        if rope:
            v = _rope(v, cos_ref[...], sin_ref[...])
        k_ref[0, :, hh * hd:(hh + 1) * hd] = v.astype(BF16)
    v0 = q0 + D_ATTN + KV_DIM
    vt_ref[0] = lax.dot_general(w_ref[0, v0:v0 + KV_DIM, :], h_scr[...], nt,
                                preferred_element_type=F32).astype(BF16)


def _cast_slabs(weights, layer, nsteps, steps_per_batch):
    specs, out_specs, out_shape = [], [], []
    for w in weights:
        rows = w.shape[1] // nsteps
        if w.shape[1] % nsteps or rows % BF16_SUBLANES or rows * w.shape[2] * 4 > CAST_SLAB_BYTES:
            return None
        blk = (1, rows, w.shape[2])
        specs.append(pl.BlockSpec(blk, lambda bi, i: (layer, bi * steps_per_batch + i, 0)))
        out_specs.append(pl.BlockSpec(blk, lambda bi, i: (0, bi * steps_per_batch + i, 0)))
        out_shape.append(jax.ShapeDtypeStruct((1,) + w.shape[1:], BF16))
    return specs, out_specs, out_shape


def _inproj(x, vecs, mods, w_in_t, tables, *, layer, cond_row, tm, cast=()):
    b, l, d = x.shape
    rope = tables is not None
    row = lambda w: pl.BlockSpec((1, tm, w), lambda bi, i: (bi, i, 0))
    tab = pl.BlockSpec((tm, LANES), lambda bi, i: (i, 0))
    widths = [(D_XBC, BF16), (D_SSM, BF16), (D_ATTN, BF16), (KV_DIM, BF16)]
    out_specs = [row(w) for w, _ in widths] + [pl.BlockSpec((1, KV_DIM, tm), lambda bi, i: (bi, 0, i)), row(LANES)]
    out_shape = ([jax.ShapeDtypeStruct((b, l, w), dt) for w, dt in widths]
                 + [jax.ShapeDtypeStruct((b, KV_DIM, l), BF16), jax.ShapeDtypeStruct((b, l, LANES), F32)])
    in_specs = [row(d), _vec_spec(vecs, layer), _mod_spec(mods, layer, M_SC_MIX), _mod_spec(mods, layer, M_SH_MIX),
                _const_spec((1,) + w_in_t.shape[1:], (layer, 0, 0))]
    args = [x, vecs, mods, mods, w_in_t]
    if rope:
        in_specs += [tab, tab]
        args += list(tables)
    if cast:
        c_in, c_out, c_shape = _cast_slabs(cast, layer, b * (l // tm), l // tm)
        in_specs += c_in
        args += list(cast)
        out_specs += c_out
        out_shape += c_shape
    return pl.pallas_call(
        functools.partial(_inproj_kernel, rope=rope, cond_row=cond_row, n_cast=len(cast)),
        grid=(b, l // tm),
        in_specs=in_specs,
        out_specs=out_specs,
        out_shape=out_shape,
        scratch_shapes=[pltpu.VMEM((tm, d), BF16)],
        compiler_params=_cparams(("arbitrary", "arbitrary")),
        name="inproj_rope" if rope else "inproj",
    )(*args)


def _conv_kernel(u_ref, prev_ref, next_ref, w_ref, v_ref, o_ref, ext_scr, *, t):
    i = pl.program_id(1)
    n = pl.num_programs(1)
    halo = SSM_CONV // 2
    h16 = BF16_SUBLANES
    zero = jnp.zeros(prev_ref.shape[1:], BF16)
    ext_scr[0:h16, :] = jnp.where(i > 0, prev_ref[0], zero)
    ext_scr[h16:h16 + t, :] = u_ref[0]
    ext_scr[h16 + t:h16 + t + h16, :] = jnp.where(i < n - 1, next_ref[0], zero)
    blk = CONV_ROWS
    win = blk + 2 * h16
    rr = lax.broadcasted_iota(jnp.int32, (blk, SSM_CONV * win), 0)
    cc = lax.broadcasted_iota(jnp.int32, (blk, SSM_CONV * win), 1)
    hit = cc == rr + h16 - halo
    for k in range(1, SSM_CONV):
        hit = hit | (cc == rr + k * win + h16 + k - halo)
    shift_all = hit.astype(F32).astype(BF16)
    taps = w_ref[0].astype(BF16)
    bias = _vec(v_ref, V_CONV_B, D_XBC)
    for r in range(t // blk):
        window = ext_scr[r * blk:r * blk + win, :]
        scaled = jnp.concatenate([window * taps[k:k + 1, :] for k in range(SSM_CONV)], axis=0)
        acc = jnp.dot(shift_all, scaled, preferred_element_type=F32) + bias
        o_ref[0, r * blk:(r + 1) * blk, :] = _silu(acc).astype(BF16)


def _conv_silu(u, conv_w, vecs, *, layer, t):
    b, l, c = u.shape
    h16 = BF16_SUBLANES
    nh = l // h16
    per = t // h16
    return pl.pallas_call(
        functools.partial(_conv_kernel, t=t),
        grid=(b, l // t),
        in_specs=[pl.BlockSpec((1, t, c), lambda bi, i: (bi, i, 0)),
                  pl.BlockSpec((1, h16, c), lambda bi, i: (bi, jnp.maximum(i * per - 1, 0), 0)),
                  pl.BlockSpec((1, h16, c), lambda bi, i: (bi, jnp.minimum((i + 1) * per, nh - 1), 0)),
                  _const_spec((1,) + conv_w.shape[1:], (layer, 0, 0)), _vec_spec(vecs, layer)],
        out_specs=pl.BlockSpec((1, t, c), lambda bi, i: (bi, i, 0)),
        out_shape=jax.ShapeDtypeStruct((b, l, c), BF16),
        scratch_shapes=[pltpu.VMEM((t + 2 * h16, c), BF16)],
        compiler_params=_cparams(("arbitrary", "arbitrary")),
        name="conv_silu",
    )(u, u, u, conv_w, vecs)


def _split3(v):
    hi = v.astype(BF16)
    r1 = v - hi.astype(F32)
    mid = r1.astype(BF16)
    lo = (r1 - mid.astype(F32)).astype(BF16)
    return hi, mid, lo


def _ssd_kernel(*refs):
    for stage in _ssd_stages(*refs):
        stage()


def _ssd_stages(xf_ref, xb_ref, dtf_ref, dtb_ref, v_ref, h0f_ref, h0b_ref,
                yf_ref, yb_ref, hff_ref, hfb_ref, ef_scr, eb_scr):
    @pl.when(pl.program_id(1) == 0)
    def _():
        r = lax.broadcasted_iota(jnp.int32, ef_scr.shape, 0) % LANES
        col = lax.broadcasted_iota(jnp.int32, ef_scr.shape, 1)
        for h0_ref, hf_ref, e_scr, off in ((h0f_ref, hff_ref, ef_scr, 0), (h0b_ref, hfb_ref, eb_scr, SSM_HEADS)):
            hf_ref[0] = h0_ref[0]
            e_scr[...] = (r == col // SSM_HEAD_DIM + off).astype(BF16)

    bias = _vec(v_ref, V_DT_BIAS, LANES)
    a_log = _vec(v_ref, V_A_LOG, LANES)
    q = SSM_CHUNK
    n = xf_ref.shape[1] // q
    order = [(j, n - 1 - j) for j in range(n)]
    preps = [(_ssd_prep(dtf_ref[0, jf * q:(jf + 1) * q, :], bias, a_log, ef_scr, rev=False),
              _ssd_prep(dtb_ref[0, jb * q:(jb + 1) * q, :], bias, a_log, eb_scr, rev=True)) for jf, jb in order]
    stages = []
    for (jf, jb), (prep_f, prep_b) in zip(order, preps):
        stages.append(functools.partial(_ssd_chunk, xf_ref, yf_ref, hff_ref, prep_f, slice(jf * q, (jf + 1) * q),
                                        rev=False))
        stages.append(functools.partial(_ssd_chunk, xb_ref, yb_ref, hfb_ref, prep_b, slice(jb * q, (jb + 1) * q),
                                        rev=True))
    return stages


def _ssd_prep(dt_raw, bias, a_log, e_scr, *, rev):
    q = SSM_CHUNK
    x = dt_raw + bias
    dt = jnp.maximum(x, 0.0) + jnp.log1p(jnp.exp(-jnp.abs(x)))
    a2 = -jnp.exp(a_log) * LOG2_E
    dta = dt * a2

    ri = lax.broadcasted_iota(jnp.int32, (q, q), 0)
    ci = lax.broadcasted_iota(jnp.int32, (q, q), 1)
    mask = (ci >= ri) if rev else (ci <= ri)
    tri = mask.astype(BF16)
    cs = sum(jnp.dot(tri, part, preferred_element_type=F32) for part in _split3(dta))
    tot = cs[0:1] if rev else cs[q - 1:q]
    row_t = (cs - jnp.log2(dt)).T

    ecs = jnp.exp2(cs)
    ecs_hi = ecs.astype(BF16)
    ecs_lo = (ecs - ecs_hi.astype(F32)).astype(BF16)
    w_end = (dt * jnp.exp2(tot - cs)).astype(BF16)
    wx = jnp.dot(w_end, e_scr[0:LANES, :], preferred_element_type=F32)
    ecsx = jnp.dot(jnp.concatenate([ecs_hi, ecs_lo], axis=1), e_scr[...], preferred_element_type=F32)
    return dict(mask=mask, cs=cs, row_t=row_t, wx=wx, ecsx=ecsx)


def _ssd_chunk(xbc_ref, y_ref, hf_ref, prep, rows, *, rev):
    q = SSM_CHUNK
    off = SSM_HEADS if rev else 0
    mask, cs, row_t, wx, ecsx = prep["mask"], prep["cs"], prep["row_t"], prep["wx"], prep["ecsx"]
    cdec = ecsx[0:1] if rev else ecsx[q - 1:q]

    lane_lo = lax.broadcasted_iota(jnp.int32, (q, LANES), 1) < SSM_HEAD_DIM
    nt = (((1,), (1,)), ((), ()))
    for g in range(SSM_GROUPS):
        gc = slice(g * GROUP_COLS, (g + 1) * GROUP_COLS)
        bg = xbc_ref[0, rows, D_SSM + g * SSM_STATE:D_SSM + (g + 1) * SSM_STATE]
        cg = xbc_ref[0, rows, D_SSM + SSM_BC + g * SSM_STATE:D_SSM + SSM_BC + (g + 1) * SSM_STATE]
        cb = lax.dot_general(cg, bg, nt, preferred_element_type=F32)
        s_in = hf_ref[0, :, gc]
        y_off = jnp.dot(cg, s_in.astype(BF16), preferred_element_type=F32) * ecsx[:, gc]
        xg = xbc_ref[0, rows, gc]
        xw = (xg.astype(F32) * wx[:, gc]).astype(BF16)
        bg_t = bg.astype(F32).T.astype(BF16)
        hf_ref[0, :, gc] = s_in * cdec[:, gc] + jnp.dot(bg_t, xw, preferred_element_type=F32)

        for jp in range(HEADS_PER_GROUP // 2):
            pair = g * (HEADS_PER_GROUP // 2) + jp

            def head_mat(h):
                seg = cs[:, off + h:off + h + 1] - row_t[off + h:off + h + 1, :]
                return (cb * jnp.exp2(jnp.where(mask, seg, -jnp.inf))).astype(BF16)

            lhs = jnp.concatenate([head_mat(2 * pair), head_mat(2 * pair + 1)], axis=1)
            xp = xbc_ref[0, rows, pair * LANES:(pair + 1) * LANES]
            zero = jnp.zeros_like(xp)
            rhs = jnp.concatenate([jnp.where(lane_lo, xp, zero), jnp.where(lane_lo, zero, xp)], axis=0)
            y_d = jnp.dot(lhs, rhs, preferred_element_type=F32)
            y_ref[0, rows, pair * LANES:(pair + 1) * LANES] = (
                y_d + y_off[:, jp * LANES:(jp + 1) * LANES]).astype(BF16)


def _call(kernel_fn, parts, name):
    grid = parts[0]["grid"]
    assert all(p["grid"] == grid for p in parts)
    cat = lambda key: [v for p in parts for v in p[key]]
    return pl.pallas_call(
        kernel_fn,
        grid=grid,
        in_specs=cat("in_specs"),
        out_specs=cat("out_specs"),
        out_shape=cat("out_shape"),
        scratch_shapes=cat("scratch"),
        compiler_params=_cparams(("arbitrary",) * len(grid)),
        name=name,
    )(*cat("args"))


def _ssd_parts(xbc, dt_raw, vecs, h0_f, h0_b, *, layer):
    b, l, _ = xbc.shape
    q = min(l, SSM_CHUNK * SSD_CHUNKS_PER_STEP)
    nc = l // q
    fwd = lambda w: pl.BlockSpec((1, q, w), lambda bi, c: (bi, c, 0))
    bwd = lambda w: pl.BlockSpec((1, q, w), lambda bi, c: (bi, nc - 1 - c, 0))
    st = pl.BlockSpec((1, SSM_STATE, D_SSM), lambda bi, c: (bi, 0, 0))
    y_shape = jax.ShapeDtypeStruct((b, l, D_SSM), BF16)
    h_shape = jax.ShapeDtypeStruct((b, SSM_STATE, D_SSM), F32)
    return dict(
        grid=(b, nc),
        in_specs=[fwd(D_XBC), bwd(D_XBC), fwd(LANES), bwd(LANES), _vec_spec(vecs, layer), st, st],
        args=[xbc, xbc, dt_raw, dt_raw, vecs, h0_f, h0_b],
        out_specs=[fwd(D_SSM), bwd(D_SSM), st, st],
        out_shape=[y_shape, y_shape, h_shape, h_shape],
        scratch=[pltpu.VMEM((2 * LANES, D_SSM), BF16), pltpu.VMEM((2 * LANES, D_SSM), BF16)])


def _ssd(xbc, dt_raw, vecs, h0_f, h0_b, *, layer):
    return _call(_ssd_kernel, [_ssd_parts(xbc, dt_raw, vecs, h0_f, h0_b, layer=layer)], "ssd")


def _scan_attn_kernel(*refs, n_ssd_in, n_attn_in):
    ssd_in, attn_in = refs[:n_ssd_in], refs[n_ssd_in:n_ssd_in + n_attn_in]
    outs = refs[n_ssd_in + n_attn_in:]
    ssd = _ssd_stages(*ssd_in, *outs[:4], *outs[5:])
    attn = _attn_stages(*attn_in, outs[4], band=True)
    for n in range(max(len(ssd), len(attn))):
        for stages in (attn, ssd):
            if n < len(stages):
                stages[n]()


def _ssd_attention(xbc, dt_raw, vecs, h0_f, h0_b, q, k, vt, kx, vtx, *, layer):
    ssd = _ssd_parts(xbc, dt_raw, vecs, h0_f, h0_b, layer=layer)
    att = _attn_parts(q, k, vt, kx, vtx, vecs, layer=layer, band=True)
    fn = functools.partial(_scan_attn_kernel, n_ssd_in=len(ssd["args"]), n_attn_in=len(att["args"]))
    return _call(fn, [ssd, att], "ssd_attn")


def _attn_stages(*refs, band):
    if band:
        q_ref, kp_ref, kc_ref, kn_ref, vp_ref, vc_ref, vn_ref, kx_ref, vx_ref, v_ref, o_ref = refs
    else:
        q_ref, kx_ref, vx_ref, v_ref, o_ref = refs
    i = pl.program_id(1)
    t = ATTN_BLOCK
    hd = ATTN_HEAD_DIM
    rep = ATTN_HEADS // ATTN_KV_HEADS
    tq = q_ref.shape[1]
    nsub = tq // t if band else 1
    rows_per = t if band else tq
    nblk = pl.num_programs(1) * nsub
    lc = kx_ref.shape[1]
    nk = (3 * t if band else 0) + lc
    nq = rep * rows_per
    nt = (((1,), (1,)), ((), ()))

    if band:
        krow = lax.broadcasted_iota(jnp.int32, (t, nq), 0)
        qcol = lax.broadcasted_iota(jnp.int32, (t, nq), 1) % t
    qhead = lax.broadcasted_iota(jnp.int32, (1, nq), 1) // rows_per
    ones = jnp.ones((BF16_SUBLANES, nk), BF16)
    sinks = _vec(v_ref, V_SINK, LANES) * LOG2_E

    def scores(g, j):
        ks = slice(g * hd, (g + 1) * hd)
        rows = slice(j * rows_per, (j + 1) * rows_per)
        qg = jnp.concatenate([q_ref[0, rows, (g * rep + r) * hd:(g * rep + r + 1) * hd] for r in range(rep)],
                             axis=0)
        if band:
            band_k = jnp.concatenate([kp_ref[0, :, ks], kc_ref[0, :, ks], kn_ref[0, :, ks]], axis=0)
            k_all = jnp.concatenate([band_k[j * t:(j + 3) * t], kx_ref[0, :, ks]], axis=0)
        else:
            k_all = kx_ref[0, :, ks]
        s = lax.dot_general(k_all, qg, nt, preferred_element_type=F32)
        if band:
            qb = i * nsub + j
            lo = qcol + jnp.where(qb > 0, 0, 2 * t)
            hi = qcol - jnp.where(qb < nblk - 1, 0, 2 * t)
            s = jnp.concatenate([jnp.where(krow >= lo, s[0:t], -jnp.inf), s[t:2 * t],
                                 jnp.where(krow <= hi, s[2 * t:3 * t], -jnp.inf), s[3 * t:]], axis=0)
        return s

    def finish(g, j, s):
        ks = slice(g * hd, (g + 1) * hd)
        rows = slice(j * rows_per, (j + 1) * rows_per)
        sink = jnp.zeros((1, nq), F32)
        for r in range(rep):
            sink = jnp.where(qhead == r, sinks[:, g * rep + r:g * rep + r + 1], sink)
        if band:
            band_v = jnp.concatenate([vp_ref[0, ks, :], vc_ref[0, ks, :], vn_ref[0, ks, :]], axis=1)
            v_all = jnp.concatenate([band_v[:, j * t:(j + 3) * t], vx_ref[0, ks, :]], axis=1)
        else:
            v_all = vx_ref[0, ks, :]
        m = jnp.maximum(jnp.max(s, axis=0, keepdims=True), sink)
        p = jnp.exp2(s - m).astype(BF16)
        ov = jnp.dot(jnp.concatenate([v_all, ones], axis=0), p, preferred_element_type=F32)
        inv = 1.0 / (ov[hd:hd + 1] + jnp.exp2(sink - m))
        o = ov[0:hd] * inv
        for r in range(rep):
            o_ref[0, rows, (g * rep + r) * hd:(g * rep + r + 1) * hd] = (
                o[:, r * rows_per:(r + 1) * rows_per].T.astype(BF16))

    chains = [(g, j) for g in range(ATTN_KV_HEADS) for j in range(nsub)]
    pending = [scores(*chains[0])]

    def stage(n):
        s_cur = pending.pop()
        if n + 1 < len(chains):
            pending.append(scores(*chains[n + 1]))
        finish(*chains[n], s_cur)

    return [functools.partial(stage, n) for n in range(len(chains))]


def _attn_kernel(*refs, band):
    for stage in _attn_stages(*refs, band=band):
        stage()


def _attn_parts(q, k, vt, kx, vtx, vecs, *, layer, band):
    b, l, _ = q.shape
    t = ATTN_BLOCK
    tq = min(l, ATTN_BLOCKS_PER_STEP * t)
    nsub = tq // t
    nblk = l // t
    lc = kx.shape[1] // b
    qs = pl.BlockSpec((1, tq, D_ATTN), lambda bi, i: (bi, i, 0))
    kctx = pl.BlockSpec((1, lc, KV_DIM), lambda bi, i: (0, bi, 0))
    vctx = pl.BlockSpec((1, KV_DIM, lc), lambda bi, i: (0, 0, bi))
    vs = _vec_spec(vecs, layer)
    if band:
        prv = lambda i: jnp.maximum(i * nsub - 1, 0)
        nxt = lambda i: jnp.minimum((i + 1) * nsub, nblk - 1)
        kspec = lambda rows, f: pl.BlockSpec((1, rows, KV_DIM), lambda bi, i: (bi, f(i), 0))
        vspec = lambda rows, f: pl.BlockSpec((1, KV_DIM, rows), lambda bi, i: (bi, 0, f(i)))
        cur = lambda i: i
        in_specs = [qs, kspec(t, prv), kspec(tq, cur), kspec(t, nxt),
                    vspec(t, prv), vspec(tq, cur), vspec(t, nxt), kctx, vctx, vs]
        args = [q, k, k, k, vt, vt, vt, kx, vtx, vecs]
    else:
        in_specs = [qs, kctx, vctx, vs]
        args = [q, kx, vtx, vecs]
    return dict(grid=(b, l // tq), in_specs=in_specs, args=args, out_specs=[qs],
                out_shape=[jax.ShapeDtypeStruct((b, l, D_ATTN), BF16)], scratch=[])


def _attention(q, k, vt, kx, vtx, vecs, *, layer, band):
    parts = _attn_parts(q, k, vt, kx, vtx, vecs, layer=layer, band=band)
    return _call(functools.partial(_attn_kernel, band=band), [parts], "attn_band" if band else "attn_ctx")[0]


def _outproj_kernel(yf_ref, yb_ref, xs_ref, z_ref, ya_ref, x_ref, w_ref, v_ref, gate_ref, scf_ref, shf_ref,
                    xo_ref, hm_ref, *, sub, cond_row):
    d_skip = _vec(v_ref, V_DSKIP, D_SSM)
    g_ssm = _vec(v_ref, V_SSM_NORM, D_SSM)
    gate = _cond(gate_ref, cond_row) * _vec(v_ref, V_G_POST_MIX)
    g_pre = _vec(v_ref, V_G_PRE_MLP) * (1.0 + _cond(scf_ref, cond_row))
    shift = _cond(shf_ref, cond_row)
    for r in range(x_ref.shape[1] // sub):
        rows = slice(r * sub, (r + 1) * sub)
        y = (yf_ref[0, rows, :].astype(F32) + yb_ref[0, rows, :].astype(F32)
             + xs_ref[0, rows, :].astype(F32) * d_skip)
        y = y * _silu(z_ref[0, rows, :].astype(F32))
        cat = jnp.concatenate([_rms(y, g_ssm).astype(BF16), ya_ref[0, rows, :]], axis=1)
        mix = jnp.dot(cat, w_ref[0], preferred_element_type=F32)
        xn = x_ref[0, rows, :] + _unit(mix) * gate
        xo_ref[0, rows, :] = xn
        hm_ref[0, rows, :] = (_unit(xn) * g_pre + shift).astype(BF16)


def _outproj(yf, yb, xbc, z, ya, x, w_out_b, vecs, mods, *, layer, w_layer, cond_row, tm):
    b, l, d = x.shape
    row = lambda w: pl.BlockSpec((1, tm, w), lambda bi, i: (bi, i, 0))
    return pl.pallas_call(
        functools.partial(_outproj_kernel, sub=min(tm, OUT_SUB_ROWS), cond_row=cond_row),
        grid=(b, l // tm),
        in_specs=[row(D_SSM), row(D_SSM), row(D_SSM), row(D_SSM), row(D_ATTN), row(d),
                  _const_spec((1,) + w_out_b.shape[1:], (w_layer, 0, 0)), _vec_spec(vecs, layer),
                  _mod_spec(mods, layer, M_G_MIX), _mod_spec(mods, layer, M_SC_MLP),
                  _mod_spec(mods, layer, M_SH_MLP)],
        out_specs=[row(d), row(d)],
        out_shape=[jax.ShapeDtypeStruct((b, l, d), F32), jax.ShapeDtypeStruct((b, l, d), BF16)],
        compiler_params=_cparams(("arbitrary", "arbitrary")),
        name="outproj",
    )(yf, yb, xbc, z, ya, x, w_out_b, vecs, mods, mods, mods)


def _mlp_kernel(h_ref, x_ref, w1_ref, w2_ref, v_ref, gate_ref, xo_ref, acc_ref, *, cond_row):
    k = pl.program_id(2)
    a = jnp.dot(h_ref[0], w1_ref[0], preferred_element_type=F32)
    a = jnp.square(jnp.maximum(a, 0.0)).astype(BF16)
    acc_ref[...] = jnp.where(k == 0, 0.0, acc_ref[...]) + jnp.dot(a, w2_ref[0], preferred_element_type=F32)

    @pl.when(k == pl.num_programs(2) - 1)
    def _():
        gate = _cond(gate_ref, cond_row) * _vec(v_ref, V_G_POST_MLP)
        xo_ref[0] = x_ref[0] + _unit(acc_ref[...]) * gate


def _mlp(h, x, w1_b, w2_b, vecs, mods, *, layer, w_layer, cond_row, tm, tf):
    b, l, d = x.shape
    dff = w1_b.shape[2]
    row = pl.BlockSpec((1, tm, d), lambda bi, i, k: (bi, i, 0))
    return pl.pallas_call(
        functools.partial(_mlp_kernel, cond_row=cond_row),
        grid=(b, l // tm, dff // tf),
        in_specs=[row, row,
                  pl.BlockSpec((1, d, tf), lambda bi, i, k: (w_layer, 0, k)),
                  pl.BlockSpec((1, tf, d), lambda bi, i, k: (w_layer, k, 0)),
                  _vec_spec(vecs, layer), _mod_spec(mods, layer, M_G_MLP)],
        out_specs=row,
        out_shape=jax.ShapeDtypeStruct((b, l, d), F32),
        scratch_shapes=[pltpu.VMEM((tm, d), F32)],
        compiler_params=_cparams(("arbitrary", "arbitrary", "arbitrary")),
        name="mlp",
    )(h, x, w1_b, w2_b, vecs, mods)


def _rope_tables(l):
    n_freq = ATTN_HEAD_DIM // 4
    inv_freq = ROPE_BASE ** (-np.arange(n_freq, dtype=np.float64) / n_freq)
    t = np.arange(l)
    ang_r = (t // GRID_W)[:, None] * inv_freq[None, :]
    ang_c = (t % GRID_W)[:, None] * inv_freq[None, :]
    cos = np.concatenate([np.cos(ang_r)] * 2 + [np.cos(ang_c)] * 2, axis=-1)
    sin = np.concatenate([-np.sin(ang_r), np.sin(ang_r), -np.sin(ang_c), np.sin(ang_c)], axis=-1)
    return jnp.asarray(cos, F32), jnp.asarray(sin, F32)


def _pack_vectors(d, g_pre_mix, g_post_mix, g_pre_mlp, g_post_mlp, ssm_norm, d_skip, conv_b, dt_bias, a_log,
                  attn_sink):
    depth = g_pre_mix.shape[0]

    def r(v):
        v = v.reshape(depth, 1, -1).astype(F32)
        return jnp.pad(v, ((0, 0), (0, 0), (0, d - v.shape[-1])))

    rows = [None] * VEC_ROWS
    rows[V_G_PRE_MIX], rows[V_G_POST_MIX] = r(g_pre_mix), r(g_post_mix)
    rows[V_G_PRE_MLP], rows[V_G_POST_MLP] = r(g_pre_mlp), r(g_post_mlp)
    rows[V_SSM_NORM], rows[V_DSKIP] = r(ssm_norm), r(jnp.repeat(d_skip, SSM_HEAD_DIM, axis=-1))
    rows[V_CONV_B], rows[V_DT_BIAS], rows[V_A_LOG], rows[V_SINK] = r(conv_b), r(dt_bias), r(a_log), r(attn_sink)
    zero = jnp.zeros((depth, 1, d), F32)
    return jnp.concatenate([zero if v is None else v for v in rows], axis=1)


def kernel(x, c, ctx, c_ctx, w_mod, b_mod, g_pre_mix, g_post_mix, g_pre_mlp, g_post_mlp, w_in, conv_w,
           conv_b, a_log, dt_bias, d_skip, ssm_norm, attn_sink, w_out, w_ff1, w_ff2):
    b, l, d = x.shape
    lc = ctx.shape[1]
    depth = w_mod.shape[0]
    assert b < COND_ROWS and l % ROW_TILE == 0 and lc % SSM_CHUNK == 0 and w_in.shape[2] == D_IN_PROJ

    cond = jnp.concatenate([c, c_ctx[None, :], jnp.zeros((COND_ROWS - b - 1, d), F32)], axis=0)
    ctx_row = b
    mods = _modulation(cond, w_mod, b_mod)
    vecs = _pack_vectors(d, g_pre_mix, g_post_mix, g_pre_mlp, g_post_mlp, ssm_norm, d_skip, conv_b, dt_bias,
                         a_log, attn_sink)
    w_in_t = jnp.swapaxes(w_in, 1, 2).astype(BF16)
    later_weights = (w_ff1, w_ff2, w_out)
    steps = b * (l // ROW_TILE)
    side_cast = _cast_slabs(later_weights, 0, steps, l // ROW_TILE) is not None
    if not side_cast:
        w1_b, w2_b, wo_b = (w.astype(BF16) for w in later_weights)
    tables = _rope_tables(l)
    h_zero = jnp.zeros((b, SSM_STATE, D_SSM), F32)

    nc = b * lc
    tm_c = min(nc, ROW_TILE)
    flat = lambda a: a.reshape(1, nc, a.shape[-1])
    per_batch = lambda a: a.reshape(b, lc, a.shape[-1])
    xc = flat(ctx)
    for i in range(depth):
        last = i == depth - 1
        xbc_c, z_c, q_c, k_c, vt_c, dt_c = _inproj(xc, vecs, mods, w_in_t, None, layer=i, cond_row=ctx_row, tm=tm_c)
        xbc_c = _conv_silu(per_batch(xbc_c), conv_w, vecs, layer=i, t=min(lc, ROW_TILE))
        yf_c, yb_c, h_f, h_b = _ssd(xbc_c, per_batch(dt_c), vecs, h_zero, h_zero, layer=i)

        outs = _inproj(x, vecs, mods, w_in_t, tables, layer=i, cond_row=None, tm=ROW_TILE,
                       cast=later_weights if side_cast else ())
        xbc_l, z_l, q_l, k_l, vt_l, dt_l = outs[:6]
        w1_i, w2_i, wo_i, w_layer = (*outs[6:9], 0) if side_cast else (w1_b, w2_b, wo_b, i)
        xbc_l = _conv_silu(xbc_l, conv_w, vecs, layer=i, t=ROW_TILE)
        yf, yb, _, _, ya = _ssd_attention(xbc_l, dt_l, vecs, h_f, h_b, q_l, k_l, vt_l, k_c, vt_c, layer=i)
        x, hm = _outproj(yf, yb, xbc_l, z_l, ya, x, wo_i, vecs, mods, layer=i, w_layer=w_layer, cond_row=None,
                         tm=ROW_TILE)
        x = _mlp(hm, x, w1_i, w2_i, vecs, mods, layer=i, w_layer=w_layer, cond_row=None, tm=ROW_TILE, tf=FF_TILE)

        if not last:
            ya_c = _attention(per_batch(q_c), None, None, k_c, vt_c, vecs, layer=i, band=False)
            xc, hm_c = _outproj(flat(yf_c), flat(yb_c), flat(xbc_c), z_c, flat(ya_c), xc, wo_i, vecs, mods,
                                layer=i, w_layer=w_layer, cond_row=ctx_row, tm=tm_c)
            xc = _mlp(hm_c, xc, w1_i, w2_i, vecs, mods, layer=i, w_layer=w_layer, cond_row=ctx_row, tm=tm_c,
                      tf=FF_TILE)
    return x
```

```python
import functools

import numpy as np
import jax
import jax.numpy as jnp
from jax import lax
from jax.experimental import pallas as pl
from jax.experimental.pallas import tpu as pltpu

F32 = jnp.float32
BF16 = jnp.bfloat16

N_MOD = 6
SSM_HEAD_DIM = 64
SSM_HEADS = 16
SSM_STATE = 128
SSM_GROUPS = 2
SSM_CONV = 5
SSM_CHUNK = 128
ATTN_HEAD_DIM = 128
ATTN_HEADS = 8
ATTN_KV_HEADS = 2
ATTN_BLOCK = 128
GRID_W = 64
ROPE_BASE = 10000.0
EPS = 1e-6
LOG2_E = 1.4426950408889634

D_SSM = SSM_HEADS * SSM_HEAD_DIM
SSM_BC = SSM_GROUPS * SSM_STATE
D_XBC = D_SSM + 2 * SSM_BC
D_ATTN = ATTN_HEADS * ATTN_HEAD_DIM
KV_DIM = ATTN_KV_HEADS * ATTN_HEAD_DIM
N_DT = 2 * SSM_HEADS
D_IN_PROJ = D_XBC + D_SSM + N_DT + D_ATTN + 2 * KV_DIM
HEADS_PER_GROUP = SSM_HEADS // SSM_GROUPS
GROUP_COLS = HEADS_PER_GROUP * SSM_HEAD_DIM

LANES = 128
F32_SUBLANES = 8
BF16_SUBLANES = 16
VMEM_LIMIT_BYTES = 56 * 1024 * 1024

ROW_TILE = 512
FF_TILE = 1024
MOD_TILE = 1024
PROJ_SEG = 512
OUT_SUB_ROWS = 256
COND_ROWS = F32_SUBLANES
CONV_ROWS = 128
SSD_CHUNKS_PER_STEP = 4
ATTN_BLOCKS_PER_STEP = 4
CAST_SLAB_BYTES = 2 * 1024 * 1024

V_G_PRE_MIX, V_G_POST_MIX, V_G_PRE_MLP, V_G_POST_MLP = 0, 1, 2, 3
V_SSM_NORM, V_DSKIP, V_CONV_B, V_DT_BIAS, V_A_LOG, V_SINK = 4, 5, 6, 7, 8, 9
VEC_ROWS = 16
M_SH_MIX, M_SC_MIX, M_G_MIX, M_SH_MLP, M_SC_MLP, M_G_MLP = range(N_MOD)


def _cparams(sem):
    return pltpu.CompilerParams(dimension_semantics=sem, vmem_limit_bytes=VMEM_LIMIT_BYTES)


def _unit(xf):
    ms = jnp.mean(xf * xf, axis=-1, keepdims=True)
    return xf * lax.rsqrt(ms + EPS)


def _rms(xf, g):
    return _unit(xf) * g


def _silu(v):
    return v * jax.nn.sigmoid(v)


def _const_spec(block, index):
    return pl.BlockSpec(block, lambda *_: index, pipeline_mode=pl.Buffered(1))


def _vec_spec(vecs, layer):
    return _const_spec((1,) + vecs.shape[1:], (layer, 0, 0))


def _mod_spec(mods, layer, chunk):
    d = mods.shape[-1] // N_MOD
    return _const_spec((1, mods.shape[1], d), (layer, 0, chunk))


def _vec(v_ref, row, width=None):
    return v_ref[0, row:row + 1, :] if width is None else v_ref[0, row:row + 1, 0:width]


def _cond(m_ref, cond_row):
    if cond_row is None:
        return m_ref[0, pl.ds(pl.program_id(0), 1), :]
    return m_ref[0, cond_row:cond_row + 1, :]


def _mod_kernel(c_ref, w_ref, b_ref, o_ref):
    s = _silu(c_ref[...]).astype(BF16)
    o_ref[0] = jnp.dot(s, w_ref[0].astype(BF16), preferred_element_type=F32) + b_ref[0]


def _modulation(cond, w_mod, b_mod):
    depth, d, n = w_mod.shape
    rows = cond.shape[0]
    tn = MOD_TILE
    return pl.pallas_call(
        _mod_kernel,
        grid=(depth, n // tn),
        in_specs=[pl.BlockSpec((rows, d), lambda l, j: (0, 0)),
                  pl.BlockSpec((1, d, tn), lambda l, j: (l, 0, j)),
                  pl.BlockSpec((1, 1, tn), lambda l, j: (l, 0, j))],
        out_specs=pl.BlockSpec((1, rows, tn), lambda l, j: (l, 0, j)),
        out_shape=jax.ShapeDtypeStruct((depth, rows, n), F32),
        compiler_params=_cparams(("arbitrary", "arbitrary")),
        name="modulation",
    )(cond, w_mod, b_mod.reshape(depth, 1, n))


def _rope(v, cos, sin):
    lane = lax.broadcasted_iota(jnp.int32, v.shape, 1)
    partner = jnp.where((lane % 64) < 32, pltpu.roll(v, 96, 1), pltpu.roll(v, 32, 1))
    return v * cos + partner * sin


def _inproj_kernel(*refs, rope, cond_row, n_cast, lean):
    n_in = 7 if rope else 5
    if rope:
        x_ref, v_ref, sc_ref, sh_ref, w_ref, cos_ref, sin_ref = refs[:n_in]
    else:
        x_ref, v_ref, sc_ref, sh_ref, w_ref = refs[:n_in]
    outs = refs[n_in + n_cast:]
    if lean:
        xbc_ref, k_ref, vt_ref, dt_ref = outs[:4]
    else:
        xbc_ref, z_ref, q_ref, k_ref, vt_ref, dt_ref = outs[:6]
    n_main = 4 if lean else 6
    h_scr = outs[-1]
    for src, dst in zip(refs[n_in:n_in + n_cast], outs[n_main:n_main + n_cast]):
        dst[...] = src[...].astype(BF16)
    h = (_rms(x_ref[0], _vec(v_ref, V_G_PRE_MIX)) * (1.0 + _cond(sc_ref, cond_row))
         + _cond(sh_ref, cond_row))
    h_scr[...] = h.astype(BF16)

    nt = (((1,), (1,)), ((), ()))

    def proj(r0, n):
        return lax.dot_general(h_scr[...], w_ref[0, r0:r0 + n, :], nt, preferred_element_type=F32)

    seg = PROJ_SEG
    for s in range(D_XBC // seg):
        xbc_ref[0, :, s * seg:(s + 1) * seg] = proj(s * seg, seg).astype(BF16)
    if not lean:
        for s in range(D_SSM // seg):
            z_ref[0, :, s * seg:(s + 1) * seg] = proj(D_XBC + s * seg, seg).astype(BF16)

    dt0 = D_XBC + D_SSM
    lane = lax.broadcasted_iota(jnp.int32, (x_ref.shape[1], LANES), 1)
    dt_ref[0] = jnp.where(lane < N_DT, proj(dt0, LANES), 0.0)

    scale = ATTN_HEAD_DIM ** -0.5 * LOG2_E
    hd = ATTN_HEAD_DIM
    q0 = dt0 + N_DT
    for s in range(0 if lean else D_ATTN // seg):
        acc = proj(q0 + s * seg, seg)
        for hh in range(seg // hd):
            v = acc[:, hh * hd:(hh + 1) * hd]
            if rope:
                v = _rope(v, cos_ref[...], sin_ref[...])
            q_ref[0, :, s * seg + hh * hd:s * seg + (hh + 1) * hd] = (v * scale).astype(BF16)
    acc = proj(q0 + D_ATTN, KV_DIM)
    for hh in range(ATTN_KV_HEADS):
        v = acc[:, hh * hd:(hh + 1) * hd]
        if rope:
            v = _rope(v, cos_ref[...], sin_ref[...])
        k_ref[0, :, hh * hd:(hh + 1) * hd] = v.astype(BF16)
    v0 = q0 + D_ATTN + KV_DIM
    vt_ref[0] = lax.dot_general(w_ref[0, v0:v0 + KV_DIM, :], h_scr[...], nt,
                                preferred_element_type=F32).astype(BF16)


def _cast_slabs(weights, layer, nsteps, steps_per_batch):
    specs, out_specs, out_shape = [], [], []
    for w in weights:
        rows = w.shape[1] // nsteps
        if w.shape[1] % nsteps or rows % BF16_SUBLANES or rows * w.shape[2] * 4 > CAST_SLAB_BYTES:
            return None
        blk = (1, rows, w.shape[2])
        specs.append(pl.BlockSpec(blk, lambda bi, i: (layer, bi * steps_per_batch + i, 0)))
        out_specs.append(pl.BlockSpec(blk, lambda bi, i: (0, bi * steps_per_batch + i, 0)))
        out_shape.append(jax.ShapeDtypeStruct((1,) + w.shape[1:], BF16))
    return specs, out_specs, out_shape


def _inproj(x, vecs, mods, w_in_t, tables, *, layer, cond_row, tm, cast=(), lean=False):
    b, l, d = x.shape
    rope = tables is not None
    row = lambda w: pl.BlockSpec((1, tm, w), lambda bi, i: (bi, i, 0))
    tab = pl.BlockSpec((tm, LANES), lambda bi, i: (i, 0))
    widths = [(D_XBC, BF16)] + ([] if lean else [(D_SSM, BF16), (D_ATTN, BF16)]) + [(KV_DIM, BF16)]
    out_specs = [row(w) for w, _ in widths] + [pl.BlockSpec((1, KV_DIM, tm), lambda bi, i: (bi, 0, i)), row(LANES)]
    out_shape = ([jax.ShapeDtypeStruct((b, l, w), dt) for w, dt in widths]
                 + [jax.ShapeDtypeStruct((b, KV_DIM, l), BF16), jax.ShapeDtypeStruct((b, l, LANES), F32)])
    in_specs = [row(d), _vec_spec(vecs, layer), _mod_spec(mods, layer, M_SC_MIX), _mod_spec(mods, layer, M_SH_MIX),
                _const_spec((1,) + w_in_t.shape[1:], (layer, 0, 0))]
    args = [x, vecs, mods, mods, w_in_t]
    if rope:
        in_specs += [tab, tab]
        args += list(tables)
    if cast:
        c_in, c_out, c_shape = _cast_slabs(cast, layer, b * (l // tm), l // tm)
        in_specs += c_in
        args += list(cast)
        out_specs += c_out
        out_shape += c_shape
    return pl.pallas_call(
        functools.partial(_inproj_kernel, rope=rope, cond_row=cond_row, n_cast=len(cast), lean=lean),
        grid=(b, l // tm),
        in_specs=in_specs,
        out_specs=out_specs,
        out_shape=out_shape,
        scratch_shapes=[pltpu.VMEM((tm, d), BF16)],
        compiler_params=_cparams(("arbitrary", "arbitrary")),
        name="inproj_rope" if rope else "inproj",
    )(*args)


def _conv_kernel(u_ref, prev_ref, next_ref, w_ref, v_ref, o_ref, ext_scr, *, t):
    i = pl.program_id(1)
    n = pl.num_programs(1)
    halo = SSM_CONV // 2
    h16 = BF16_SUBLANES
    zero = jnp.zeros(prev_ref.shape[1:], BF16)
    ext_scr[0:h16, :] = jnp.where(i > 0, prev_ref[0], zero)
    ext_scr[h16:h16 + t, :] = u_ref[0]
    ext_scr[h16 + t:h16 + t + h16, :] = jnp.where(i < n - 1, next_ref[0], zero)
    blk = CONV_ROWS
    win = blk + 2 * h16
    rr = lax.broadcasted_iota(jnp.int32, (blk, SSM_CONV * win), 0)
    cc = lax.broadcasted_iota(jnp.int32, (blk, SSM_CONV * win), 1)
    hit = cc == rr + h16 - halo
    for k in range(1, SSM_CONV):
        hit = hit | (cc == rr + k * win + h16 + k - halo)
    shift_all = hit.astype(F32).astype(BF16)
    taps = w_ref[0].astype(BF16)
    bias = _vec(v_ref, V_CONV_B, D_XBC)
    for r in range(t // blk):
        window = ext_scr[r * blk:r * blk + win, :]
        scaled = jnp.concatenate([window * taps[k:k + 1, :] for k in range(SSM_CONV)], axis=0)
        acc = jnp.dot(shift_all, scaled, preferred_element_type=F32) + bias
        o_ref[0, r * blk:(r + 1) * blk, :] = _silu(acc).astype(BF16)


def _conv_silu(u, conv_w, vecs, *, layer, t):
    b, l, c = u.shape
    h16 = BF16_SUBLANES
    nh = l // h16
    per = t // h16
    return pl.pallas_call(
        functools.partial(_conv_kernel, t=t),
        grid=(b, l // t),
        in_specs=[pl.BlockSpec((1, t, c), lambda bi, i: (bi, i, 0)),
                  pl.BlockSpec((1, h16, c), lambda bi, i: (bi, jnp.maximum(i * per - 1, 0), 0)),
                  pl.BlockSpec((1, h16, c), lambda bi, i: (bi, jnp.minimum((i + 1) * per, nh - 1), 0)),
                  _const_spec((1,) + conv_w.shape[1:], (layer, 0, 0)), _vec_spec(vecs, layer)],
        out_specs=pl.BlockSpec((1, t, c), lambda bi, i: (bi, i, 0)),
        out_shape=jax.ShapeDtypeStruct((b, l, c), BF16),
        scratch_shapes=[pltpu.VMEM((t + 2 * h16, c), BF16)],
        compiler_params=_cparams(("arbitrary", "arbitrary")),
        name="conv_silu",
    )(u, u, u, conv_w, vecs)


def _split3(v):
    hi = v.astype(BF16)
    r1 = v - hi.astype(F32)
    mid = r1.astype(BF16)
    lo = (r1 - mid.astype(F32)).astype(BF16)
    return hi, mid, lo


def _ssd_kernel(*refs, want_y=True):
    for stage in _ssd_stages(*refs, want_y=want_y):
        stage()


def _ssd_stages(*refs, want_y=True):
    xf_ref, xb_ref, dtf_ref, dtb_ref, v_ref, h0f_ref, h0b_ref = refs[:7]
    yf_ref, yb_ref = refs[7:9] if want_y else (None, None)
    hff_ref, hfb_ref, ef_scr, eb_scr = refs[-4:]
    @pl.when(pl.program_id(1) == 0)
    def _():
        r = lax.broadcasted_iota(jnp.int32, ef_scr.shape, 0) % LANES
        col = lax.broadcasted_iota(jnp.int32, ef_scr.shape, 1)
        for h0_ref, hf_ref, e_scr, off in ((h0f_ref, hff_ref, ef_scr, 0), (h0b_ref, hfb_ref, eb_scr, SSM_HEADS)):
            hf_ref[0] = h0_ref[0]
            e_scr[...] = (r == col // SSM_HEAD_DIM + off).astype(BF16)

    bias = _vec(v_ref, V_DT_BIAS, LANES)
    a_log = _vec(v_ref, V_A_LOG, LANES)
    q = SSM_CHUNK
    n = xf_ref.shape[1] // q
    order = [(j, n - 1 - j) for j in range(n)]
    preps = [(_ssd_prep(dtf_ref[0, jf * q:(jf + 1) * q, :], bias, a_log, ef_scr, rev=False),
              _ssd_prep(dtb_ref[0, jb * q:(jb + 1) * q, :], bias, a_log, eb_scr, rev=True)) for jf, jb in order]
    stages = []
    for (jf, jb), (prep_f, prep_b) in zip(order, preps):
        stages.append(functools.partial(_ssd_chunk, xf_ref, yf_ref, hff_ref, prep_f, slice(jf * q, (jf + 1) * q),
                                        rev=False))
        stages.append(functools.partial(_ssd_chunk, xb_ref, yb_ref, hfb_ref, prep_b, slice(jb * q, (jb + 1) * q),
                                        rev=True))
    return stages


def _ssd_prep(dt_raw, bias, a_log, e_scr, *, rev):
    q = SSM_CHUNK
    x = dt_raw + bias
    dt = jnp.maximum(x, 0.0) + jnp.log1p(jnp.exp(-jnp.abs(x)))
    a2 = -jnp.exp(a_log) * LOG2_E
    dta = dt * a2

    ri = lax.broadcasted_iota(jnp.int32, (q, q), 0)
    ci = lax.broadcasted_iota(jnp.int32, (q, q), 1)
    mask = (ci >= ri) if rev else (ci <= ri)
    tri = mask.astype(BF16)
    cs = sum(jnp.dot(tri, part, preferred_element_type=F32) for part in _split3(dta))
    tot = cs[0:1] if rev else cs[q - 1:q]
    row_t = (cs - jnp.log2(dt)).T

    ecs = jnp.exp2(cs)
    ecs_hi = ecs.astype(BF16)
    ecs_lo = (ecs - ecs_hi.astype(F32)).astype(BF16)
    w_end = (dt * jnp.exp2(tot - cs)).astype(BF16)
    wx = jnp.dot(w_end, e_scr[0:LANES, :], preferred_element_type=F32)
    ecsx = jnp.dot(jnp.concatenate([ecs_hi, ecs_lo], axis=1), e_scr[...], preferred_element_type=F32)
    return dict(mask=mask, cs=cs, row_t=row_t, wx=wx, ecsx=ecsx)


def _ssd_chunk(xbc_ref, y_ref, hf_ref, prep, rows, *, rev):
    q = SSM_CHUNK
    off = SSM_HEADS if rev else 0
    mask, cs, row_t, wx, ecsx = prep["mask"], prep["cs"], prep["row_t"], prep["wx"], prep["ecsx"]
    cdec = ecsx[0:1] if rev else ecsx[q - 1:q]

    lane_lo = lax.broadcasted_iota(jnp.int32, (q, LANES), 1) < SSM_HEAD_DIM
    nt = (((1,), (1,)), ((), ()))
    for g in range(SSM_GROUPS):
        gc = slice(g * GROUP_COLS, (g + 1) * GROUP_COLS)
        bg = xbc_ref[0, rows, D_SSM + g * SSM_STATE:D_SSM + (g + 1) * SSM_STATE]
        s_in = hf_ref[0, :, gc]
        xg = xbc_ref[0, rows, gc]
        xw = (xg.astype(F32) * wx[:, gc]).astype(BF16)
        bg_t = bg.astype(F32).T.astype(BF16)
        hf_ref[0, :, gc] = s_in * cdec[:, gc] + jnp.dot(bg_t, xw, preferred_element_type=F32)
        if y_ref is None:
            continue

        cg = xbc_ref[0, rows, D_SSM + SSM_BC + g * SSM_STATE:D_SSM + SSM_BC + (g + 1) * SSM_STATE]
        cb = lax.dot_general(cg, bg, nt, preferred_element_type=F32)
        y_off = jnp.dot(cg, s_in.astype(BF16), preferred_element_type=F32) * ecsx[:, gc]
        for jp in range(HEADS_PER_GROUP // 2):
            pair = g * (HEADS_PER_GROUP // 2) + jp

            def head_mat(h):
                seg = cs[:, off + h:off + h + 1] - row_t[off + h:off + h + 1, :]
                return (cb * jnp.exp2(jnp.where(mask, seg, -jnp.inf))).astype(BF16)

            lhs = jnp.concatenate([head_mat(2 * pair), head_mat(2 * pair + 1)], axis=1)
            xp = xbc_ref[0, rows, pair * LANES:(pair + 1) * LANES]
            zero = jnp.zeros_like(xp)
            rhs = jnp.concatenate([jnp.where(lane_lo, xp, zero), jnp.where(lane_lo, zero, xp)], axis=0)
            y_d = jnp.dot(lhs, rhs, preferred_element_type=F32)
            y_ref[0, rows, pair * LANES:(pair + 1) * LANES] = (
                y_d + y_off[:, jp * LANES:(jp + 1) * LANES]).astype(BF16)


def _call(kernel_fn, parts, name):
    grid = parts[0]["grid"]
    assert all(p["grid"] == grid for p in parts)
    cat = lambda key: [v for p in parts for v in p[key]]
    return pl.pallas_call(
        kernel_fn,
        grid=grid,
        in_specs=cat("in_specs"),
        out_specs=cat("out_specs"),
        out_shape=cat("out_shape"),
        scratch_shapes=cat("scratch"),
        compiler_params=_cparams(("arbitrary",) * len(grid)),
        name=name,
    )(*cat("args"))


def _ssd_parts(xbc, dt_raw, vecs, h0_f, h0_b, *, layer, want_y=True):
    b, l, _ = xbc.shape
    q = min(l, SSM_CHUNK * SSD_CHUNKS_PER_STEP)
    nc = l // q
    fwd = lambda w: pl.BlockSpec((1, q, w), lambda bi, c: (bi, c, 0))
    bwd = lambda w: pl.BlockSpec((1, q, w), lambda bi, c: (bi, nc - 1 - c, 0))
    st = pl.BlockSpec((1, SSM_STATE, D_SSM), lambda bi, c: (bi, 0, 0))
    y_shape = jax.ShapeDtypeStruct((b, l, D_SSM), BF16)
    h_shape = jax.ShapeDtypeStruct((b, SSM_STATE, D_SSM), F32)
    return dict(
        grid=(b, nc),
        in_specs=[fwd(D_XBC), bwd(D_XBC), fwd(LANES), bwd(LANES), _vec_spec(vecs, layer), st, st],
        args=[xbc, xbc, dt_raw, dt_raw, vecs, h0_f, h0_b],
        out_specs=([fwd(D_SSM), bwd(D_SSM)] if want_y else []) + [st, st],
        out_shape=([y_shape, y_shape] if want_y else []) + [h_shape, h_shape],
        scratch=[pltpu.VMEM((2 * LANES, D_SSM), BF16), pltpu.VMEM((2 * LANES, D_SSM), BF16)])


def _ssd(xbc, dt_raw, vecs, h0_f, h0_b, *, layer, want_y=True):
    parts = _ssd_parts(xbc, dt_raw, vecs, h0_f, h0_b, layer=layer, want_y=want_y)
    return _call(functools.partial(_ssd_kernel, want_y=want_y), [parts], "ssd" if want_y else "ssd_state")


def _scan_attn_kernel(*refs, n_ssd_in, n_attn_in):
    ssd_in, attn_in = refs[:n_ssd_in], refs[n_ssd_in:n_ssd_in + n_attn_in]
    outs = refs[n_ssd_in + n_attn_in:]
    ssd = _ssd_stages(*ssd_in, *outs[:4], *outs[5:])
    attn = _attn_stages(*attn_in, outs[4], band=True)
    for n in range(max(len(ssd), len(attn))):
        for stages in (attn, ssd):
            if n < len(stages):
                stages[n]()


def _ssd_attention(xbc, dt_raw, vecs, h0_f, h0_b, q, k, vt, kx, vtx, *, layer):
    ssd = _ssd_parts(xbc, dt_raw, vecs, h0_f, h0_b, layer=layer)
    att = _attn_parts(q, k, vt, kx, vtx, vecs, layer=layer, band=True)
    fn = functools.partial(_scan_attn_kernel, n_ssd_in=len(ssd["args"]), n_attn_in=len(att["args"]))
    return _call(fn, [ssd, att], "ssd_attn")


def _attn_stages(*refs, band):
    if band:
        q_ref, kp_ref, kc_ref, kn_ref, vp_ref, vc_ref, vn_ref, kx_ref, vx_ref, v_ref, o_ref = refs
    else:
        q_ref, kx_ref, vx_ref, v_ref, o_ref = refs
    i = pl.program_id(1)
    t = ATTN_BLOCK
    hd = ATTN_HEAD_DIM
    rep = ATTN_HEADS // ATTN_KV_HEADS
    tq = q_ref.shape[1]
    nsub = tq // t if band else 1
    rows_per = t if band else tq
    nblk = pl.num_programs(1) * nsub
    lc = kx_ref.shape[1]
    nk = (3 * t if band else 0) + lc
    nq = rep * rows_per
    nt = (((1,), (1,)), ((), ()))

    if band:
        krow = lax.broadcasted_iota(jnp.int32, (t, nq), 0)
        qcol = lax.broadcasted_iota(jnp.int32, (t, nq), 1) % t
    qhead = lax.broadcasted_iota(jnp.int32, (1, nq), 1) // rows_per
    ones = jnp.ones((BF16_SUBLANES, nk), BF16)
    sinks = _vec(v_ref, V_SINK, LANES) * LOG2_E

    def scores(g, j):
        ks = slice(g * hd, (g + 1) * hd)
        rows = slice(j * rows_per, (j + 1) * rows_per)
        qg = jnp.concatenate([q_ref[0, rows, (g * rep + r) * hd:(g * rep + r + 1) * hd] for r in range(rep)],
                             axis=0)
        if band:
            band_k = jnp.concatenate([kp_ref[0, :, ks], kc_ref[0, :, ks], kn_ref[0, :, ks]], axis=0)
            k_all = jnp.concatenate([band_k[j * t:(j + 3) * t], kx_ref[0, :, ks]], axis=0)
        else:
            k_all = kx_ref[0, :, ks]
        s = lax.dot_general(k_all, qg, nt, preferred_element_type=F32)
        if band:
            qb = i * nsub + j
            lo = qcol + jnp.where(qb > 0, 0, 2 * t)
            hi = qcol - jnp.where(qb < nblk - 1, 0, 2 * t)
            s = jnp.concatenate([jnp.where(krow >= lo, s[0:t], -jnp.inf), s[t:2 * t],
                                 jnp.where(krow <= hi, s[2 * t:3 * t], -jnp.inf), s[3 * t:]], axis=0)
        return s

    def finish(g, j, s):
        ks = slice(g * hd, (g + 1) * hd)
        rows = slice(j * rows_per, (j + 1) * rows_per)
        sink = jnp.zeros((1, nq), F32)
        for r in range(rep):
            sink = jnp.where(qhead == r, sinks[:, g * rep + r:g * rep + r + 1], sink)
        if band:
            band_v = jnp.concatenate([vp_ref[0, ks, :], vc_ref[0, ks, :], vn_ref[0, ks, :]], axis=1)
            v_all = jnp.concatenate([band_v[:, j * t:(j + 3) * t], vx_ref[0, ks, :]], axis=1)
        else:
            v_all = vx_ref[0, ks, :]
        m = jnp.maximum(jnp.max(s, axis=0, keepdims=True), sink)
        p = jnp.exp2(s - m).astype(BF16)
        ov = jnp.dot(jnp.concatenate([v_all, ones], axis=0), p, preferred_element_type=F32)
        inv = 1.0 / (ov[hd:hd + 1] + jnp.exp2(sink - m))
        o = ov[0:hd] * inv
        for r in range(rep):
            o_ref[0, rows, (g * rep + r) * hd:(g * rep + r + 1) * hd] = (
                o[:, r * rows_per:(r + 1) * rows_per].T.astype(BF16))

    chains = [(g, j) for g in range(ATTN_KV_HEADS) for j in range(nsub)]
    pending = [scores(*chains[0])]

    def stage(n):
        s_cur = pending.pop()
        if n + 1 < len(chains):
            pending.append(scores(*chains[n + 1]))
        finish(*chains[n], s_cur)

    return [functools.partial(stage, n) for n in range(len(chains))]


def _attn_kernel(*refs, band):
    for stage in _attn_stages(*refs, band=band):
        stage()


def _attn_parts(q, k, vt, kx, vtx, vecs, *, layer, band):
    b, l, _ = q.shape
    t = ATTN_BLOCK
    tq = min(l, ATTN_BLOCKS_PER_STEP * t)
    nsub = tq // t
    nblk = l // t
    lc = kx.shape[1] // b
    qs = pl.BlockSpec((1, tq, D_ATTN), lambda bi, i: (bi, i, 0))
    kctx = pl.BlockSpec((1, lc, KV_DIM), lambda bi, i: (0, bi, 0))
    vctx = pl.BlockSpec((1, KV_DIM, lc), lambda bi, i: (0, 0, bi))
    vs = _vec_spec(vecs, layer)
    if band:
        prv = lambda i: jnp.maximum(i * nsub - 1, 0)
        nxt = lambda i: jnp.minimum((i + 1) * nsub, nblk - 1)
        kspec = lambda rows, f: pl.BlockSpec((1, rows, KV_DIM), lambda bi, i: (bi, f(i), 0))
        vspec = lambda rows, f: pl.BlockSpec((1, KV_DIM, rows), lambda bi, i: (bi, 0, f(i)))
        cur = lambda i: i
        in_specs = [qs, kspec(t, prv), kspec(tq, cur), kspec(t, nxt),
                    vspec(t, prv), vspec(tq, cur), vspec(t, nxt), kctx, vctx, vs]
        args = [q, k, k, k, vt, vt, vt, kx, vtx, vecs]
    else:
        in_specs = [qs, kctx, vctx, vs]
        args = [q, kx, vtx, vecs]
    return dict(grid=(b, l // tq), in_specs=in_specs, args=args, out_specs=[qs],
                out_shape=[jax.ShapeDtypeStruct((b, l, D_ATTN), BF16)], scratch=[])


def _attention(q, k, vt, kx, vtx, vecs, *, layer, band):
    parts = _attn_parts(q, k, vt, kx, vtx, vecs, layer=layer, band=band)
    return _call(functools.partial(_attn_kernel, band=band), [parts], "attn_band" if band else "attn_ctx")[0]


def _outproj_kernel(yf_ref, yb_ref, xs_ref, z_ref, ya_ref, x_ref, w_ref, v_ref, gate_ref, scf_ref, shf_ref,
                    xo_ref, hm_ref, *, sub, cond_row):
    d_skip = _vec(v_ref, V_DSKIP, D_SSM)
    g_ssm = _vec(v_ref, V_SSM_NORM, D_SSM)
    gate = _cond(gate_ref, cond_row) * _vec(v_ref, V_G_POST_MIX)
    g_pre = _vec(v_ref, V_G_PRE_MLP) * (1.0 + _cond(scf_ref, cond_row))
    shift = _cond(shf_ref, cond_row)
    for r in range(x_ref.shape[1] // sub):
        rows = slice(r * sub, (r + 1) * sub)
        y = (yf_ref[0, rows, :].astype(F32) + yb_ref[0, rows, :].astype(F32)
             + xs_ref[0, rows, :].astype(F32) * d_skip)
        y = y * _silu(z_ref[0, rows, :].astype(F32))
        cat = jnp.concatenate([_rms(y, g_ssm).astype(BF16), ya_ref[0, rows, :]], axis=1)
        mix = jnp.dot(cat, w_ref[0], preferred_element_type=F32)
        xn = x_ref[0, rows, :] + _unit(mix) * gate
        xo_ref[0, rows, :] = xn
        hm_ref[0, rows, :] = (_unit(xn) * g_pre + shift).astype(BF16)


def _outproj(yf, yb, xbc, z, ya, x, w_out_b, vecs, mods, *, layer, w_layer, cond_row, tm):
    b, l, d = x.shape
    row = lambda w: pl.BlockSpec((1, tm, w), lambda bi, i: (bi, i, 0))
    return pl.pallas_call(
        functools.partial(_outproj_kernel, sub=min(tm, OUT_SUB_ROWS), cond_row=cond_row),
        grid=(b, l // tm),
        in_specs=[row(D_SSM), row(D_SSM), row(D_SSM), row(D_SSM), row(D_ATTN), row(d),
                  _const_spec((1,) + w_out_b.shape[1:], (w_layer, 0, 0)), _vec_spec(vecs, layer),
                  _mod_spec(mods, layer, M_G_MIX), _mod_spec(mods, layer, M_SC_MLP),
                  _mod_spec(mods, layer, M_SH_MLP)],
        out_specs=[row(d), row(d)],
        out_shape=[jax.ShapeDtypeStruct((b, l, d), F32), jax.ShapeDtypeStruct((b, l, d), BF16)],
        compiler_params=_cparams(("arbitrary", "arbitrary")),
        name="outproj",
    )(yf, yb, xbc, z, ya, x, w_out_b, vecs, mods, mods, mods)


def _mlp_kernel(h_ref, x_ref, w1_ref, w2_ref, v_ref, gate_ref, xo_ref, acc_ref, *, cond_row):
    k = pl.program_id(2)
    a = jnp.dot(h_ref[0], w1_ref[0], preferred_element_type=F32)
    a = jnp.square(jnp.maximum(a, 0.0)).astype(BF16)
    acc_ref[...] = jnp.where(k == 0, 0.0, acc_ref[...]) + jnp.dot(a, w2_ref[0], preferred_element_type=F32)

    @pl.when(k == pl.num_programs(2) - 1)
    def _():
        gate = _cond(gate_ref, cond_row) * _vec(v_ref, V_G_POST_MLP)
        xo_ref[0] = x_ref[0] + _unit(acc_ref[...]) * gate


def _mlp(h, x, w1_b, w2_b, vecs, mods, *, layer, w_layer, cond_row, tm, tf):
    b, l, d = x.shape
    dff = w1_b.shape[2]
    row = pl.BlockSpec((1, tm, d), lambda bi, i, k: (bi, i, 0))
    return pl.pallas_call(
        functools.partial(_mlp_kernel, cond_row=cond_row),
        grid=(b, l // tm, dff // tf),
        in_specs=[row, row,
                  pl.BlockSpec((1, d, tf), lambda bi, i, k: (w_layer, 0, k)),
                  pl.BlockSpec((1, tf, d), lambda bi, i, k: (w_layer, k, 0)),
                  _vec_spec(vecs, layer), _mod_spec(mods, layer, M_G_MLP)],
        out_specs=row,
        out_shape=jax.ShapeDtypeStruct((b, l, d), F32),
        scratch_shapes=[pltpu.VMEM((tm, d), F32)],
        compiler_params=_cparams(("arbitrary", "arbitrary", "arbitrary")),
        name="mlp",
    )(h, x, w1_b, w2_b, vecs, mods)


def _rope_tables(l):
    n_freq = ATTN_HEAD_DIM // 4
    inv_freq = ROPE_BASE ** (-np.arange(n_freq, dtype=np.float64) / n_freq)
    t = np.arange(l)
    ang_r = (t // GRID_W)[:, None] * inv_freq[None, :]
    ang_c = (t % GRID_W)[:, None] * inv_freq[None, :]
    cos = np.concatenate([np.cos(ang_r)] * 2 + [np.cos(ang_c)] * 2, axis=-1)
    sin = np.concatenate([-np.sin(ang_r), np.sin(ang_r), -np.sin(ang_c), np.sin(ang_c)], axis=-1)
    return jnp.asarray(cos, F32), jnp.asarray(sin, F32)


def _pack_vectors(d, g_pre_mix, g_post_mix, g_pre_mlp, g_post_mlp, ssm_norm, d_skip, conv_b, dt_bias, a_log,
                  attn_sink):
    depth = g_pre_mix.shape[0]

    def r(v):
        v = v.reshape(depth, 1, -1).astype(F32)
        return jnp.pad(v, ((0, 0), (0, 0), (0, d - v.shape[-1])))

    rows = [None] * VEC_ROWS
    rows[V_G_PRE_MIX], rows[V_G_POST_MIX] = r(g_pre_mix), r(g_post_mix)
    rows[V_G_PRE_MLP], rows[V_G_POST_MLP] = r(g_pre_mlp), r(g_post_mlp)
    rows[V_SSM_NORM], rows[V_DSKIP] = r(ssm_norm), r(jnp.repeat(d_skip, SSM_HEAD_DIM, axis=-1))
    rows[V_CONV_B], rows[V_DT_BIAS], rows[V_A_LOG], rows[V_SINK] = r(conv_b), r(dt_bias), r(a_log), r(attn_sink)
    zero = jnp.zeros((depth, 1, d), F32)
    return jnp.concatenate([zero if v is None else v for v in rows], axis=1)


def kernel(x, c, ctx, c_ctx, w_mod, b_mod, g_pre_mix, g_post_mix, g_pre_mlp, g_post_mlp, w_in, conv_w,
           conv_b, a_log, dt_bias, d_skip, ssm_norm, attn_sink, w_out, w_ff1, w_ff2):
    b, l, d = x.shape
    lc = ctx.shape[1]
    depth = w_mod.shape[0]
    assert b < COND_ROWS and l % ROW_TILE == 0 and lc % SSM_CHUNK == 0 and w_in.shape[2] == D_IN_PROJ

    cond = jnp.concatenate([c, c_ctx[None, :], jnp.zeros((COND_ROWS - b - 1, d), F32)], axis=0)
    ctx_row = b
    mods = _modulation(cond, w_mod, b_mod)
    vecs = _pack_vectors(d, g_pre_mix, g_post_mix, g_pre_mlp, g_post_mlp, ssm_norm, d_skip, conv_b, dt_bias,
                         a_log, attn_sink)
    w_in_t = jnp.swapaxes(w_in, 1, 2).astype(BF16)
    later_weights = (w_ff1, w_ff2, w_out)
    steps = b * (l // ROW_TILE)
    side_cast = _cast_slabs(later_weights, 0, steps, l // ROW_TILE) is not None
    if not side_cast:
        w1_b, w2_b, wo_b = (w.astype(BF16) for w in later_weights)
    tables = _rope_tables(l)
    h_zero = jnp.zeros((b, SSM_STATE, D_SSM), F32)

    nc = b * lc
    tm_c = min(nc, ROW_TILE)
    flat = lambda a: a.reshape(1, nc, a.shape[-1])
    per_batch = lambda a: a.reshape(b, lc, a.shape[-1])
    xc = flat(ctx)
    for i in range(depth):
        last = i == depth - 1
        ctx_proj = _inproj(xc, vecs, mods, w_in_t, None, layer=i, cond_row=ctx_row, tm=tm_c, lean=last)
        xbc_c, k_c, vt_c, dt_c = ctx_proj[0], ctx_proj[-3], ctx_proj[-2], ctx_proj[-1]
        xbc_c = _conv_silu(per_batch(xbc_c), conv_w, vecs, layer=i, t=min(lc, ROW_TILE))
        ctx_scan = _ssd(xbc_c, per_batch(dt_c), vecs, h_zero, h_zero, layer=i, want_y=not last)
        h_f, h_b = ctx_scan[-2], ctx_scan[-1]

        outs = _inproj(x, vecs, mods, w_in_t, tables, layer=i, cond_row=None, tm=ROW_TILE,
                       cast=later_weights if side_cast else ())
        xbc_l, z_l, q_l, k_l, vt_l, dt_l = outs[:6]
        w1_i, w2_i, wo_i, w_layer = (*outs[6:9], 0) if side_cast else (w1_b, w2_b, wo_b, i)
        xbc_l = _conv_silu(xbc_l, conv_w, vecs, layer=i, t=ROW_TILE)
        yf, yb, _, _, ya = _ssd_attention(xbc_l, dt_l, vecs, h_f, h_b, q_l, k_l, vt_l, k_c, vt_c, layer=i)
        x, hm = _outproj(yf, yb, xbc_l, z_l, ya, x, wo_i, vecs, mods, layer=i, w_layer=w_layer, cond_row=None,
                         tm=ROW_TILE)
        x = _mlp(hm, x, w1_i, w2_i, vecs, mods, layer=i, w_layer=w_layer, cond_row=None, tm=ROW_TILE, tf=FF_TILE)

        if not last:
            z_c, q_c = ctx_proj[1], ctx_proj[2]
            yf_c, yb_c = ctx_scan[0], ctx_scan[1]
            ya_c = _attention(per_batch(q_c), None, None, k_c, vt_c, vecs, layer=i, band=False)
            xc, hm_c = _outproj(flat(yf_c), flat(yb_c), flat(xbc_c), z_c, flat(ya_c), xc, wo_i, vecs, mods,
                                layer=i, w_layer=w_layer, cond_row=ctx_row, tm=tm_c)
            xc = _mlp(hm_c, xc, w1_i, w2_i, vecs, mods, layer=i, w_layer=w_layer, cond_row=ctx_row, tm=tm_c,
                      tf=FF_TILE)
    return x
```
